```python
import jax
import jax.numpy as jnp
from jax import lax
import numpy as np

D_MODEL = 1024
BATCH = 32
SEQ = 2048
DEPTH = 1
DEC_BATCH = 32
DEC_SEQ = 64
PAST_LEN = 1024

CHUNK = 64
WINDOW = 128
WINDOW_CHUNKS = WINDOW // CHUNK
D_MIX = D_MODEL
ATT_WIDTH = D_MIX // 2
N_ATT_HEADS = 8
HEAD_DIM = ATT_WIDTH // N_ATT_HEADS
N_KV_HEADS = 2
GQA_GROUP = N_ATT_HEADS // N_KV_HEADS
LSTM_WIDTH = D_MIX - ATT_WIDTH
N_LSTM_HEADS = 4
LSTM_HEAD_DIM = LSTM_WIDTH // N_LSTM_HEADS
PROJ_WIDTHS = (ATT_WIDTH, N_KV_HEADS * HEAD_DIM, N_KV_HEADS * HEAD_DIM,
               LSTM_WIDTH, LSTM_WIDTH, LSTM_WIDTH, LSTM_WIDTH, N_LSTM_HEADS, N_LSTM_HEADS)
D_IN_PROJ = sum(PROJ_WIDTHS)
D_FF = 2816
EPS = 1e-6

kernel_name = "hymba_swa_sink_mlstm_macaron_step"


def _rmsnorm(x, g):
    xf = x.astype(jnp.float32)
    y = xf * lax.rsqrt(jnp.mean(xf * xf, axis=-1, keepdims=True) + EPS)
    return (y * g.astype(jnp.float32)).astype(x.dtype)


def _swiglu_half(x, g, w_gate, w_up, w_down):
    h = _rmsnorm(x, g)
    return 0.5 * ((jax.nn.silu(h @ w_gate) * (h @ w_up)) @ w_down)


def _alibi_slopes():
    h = jnp.arange(1, N_ATT_HEADS + 1, dtype=jnp.float32)
    return jnp.exp2(-8.0 * h / N_ATT_HEADS).reshape(N_KV_HEADS, GQA_GROUP)


def _in_proj(x, g_mix, w_in, b_i, b_f):
    h = _rmsnorm(x, g_mix)
    z = h @ w_in
    q, k, v, mq, mk, mv, mo, ig, fg = jnp.split(z, np.cumsum(PROJ_WIDTHS)[:-1].tolist(), axis=-1)
    lead = x.shape[:-1]
    return (q.reshape(lead + (N_ATT_HEADS, HEAD_DIM)),
            k.reshape(lead + (N_KV_HEADS, HEAD_DIM)),
            v.reshape(lead + (N_KV_HEADS, HEAD_DIM)),
            mq.reshape(lead + (N_LSTM_HEADS, LSTM_HEAD_DIM)),
            mk.reshape(lead + (N_LSTM_HEADS, LSTM_HEAD_DIM)),
            mv.reshape(lead + (N_LSTM_HEADS, LSTM_HEAD_DIM)),
            mo.reshape(lead + (N_LSTM_HEADS, LSTM_HEAD_DIM)),
            (ig + b_i).astype(jnp.float32),
            jax.nn.log_sigmoid((fg + b_f).astype(jnp.float32)))


def _sink_attention(q, k, v, dist, valid, sinks):
    s = jnp.einsum("bcqkgd,bcskd->bckgqs", q, k).astype(jnp.float32) * (HEAD_DIM ** -0.5)
    s = s - _alibi_slopes()[:, :, None, None] * dist.astype(jnp.float32)
    s = jnp.where(valid[None, :, None, None, None, :], s, -jnp.inf)
    sink = sinks.astype(jnp.float32).reshape(N_KV_HEADS, GQA_GROUP, 1)
    mx = jnp.maximum(jnp.max(s, axis=-1), sink)
    p = jnp.exp(s - mx[..., None])
    den = jnp.sum(p, axis=-1) + jnp.exp(sink - mx)
    w = (p / den[..., None]).astype(v.dtype)
    return jnp.einsum("bckgqs,bcskd->bcqkgd", w, v)


def _mlstm_scan(q, k, v, log_i, log_f, C0, n0, m0, block):
    bsz, seq = q.shape[:2]
    nb = seq // block

    def blocks(t):
        t = t.astype(jnp.float32).reshape((bsz, nb, block) + t.shape[2:])
        return jnp.moveaxis(t, 1, 0)

    xs = (blocks(q), blocks(k * (LSTM_HEAD_DIM ** -0.5)), blocks(v), blocks(log_i), blocks(log_f))
    causal = jnp.tril(jnp.ones((block, block), dtype=bool))

    def step(carry, inp):
        C, n, m = carry
        qc, kc, vc, ic, fc = inp
        b = jnp.cumsum(fc, axis=1)
        d = b[:, :, None, :] - b[:, None, :, :] + ic[:, None, :, :]
        d = jnp.where(causal[None, :, :, None], d, -jnp.inf)
        inter = b + m[:, None, :]
        mt = jnp.maximum(inter, jnp.max(d, axis=2))
        a = jnp.exp(d - mt[:, :, None, :]) * jnp.einsum("bthd,bshd->btsh", qc, kc)
        w_inter = jnp.exp(inter - mt)
        num = (jnp.einsum("btsh,bshd->bthd", a, vc)
               + w_inter[..., None] * jnp.einsum("bthd,bhde->bthe", qc, C))
        den = jnp.sum(a, axis=2) + w_inter * jnp.einsum("bthd,bhd->bth", qc, n)
        h = num / jnp.maximum(jnp.abs(den), jnp.exp(-mt))[..., None]
        tot = b[:, -1]
        g = tot[:, None, :] - b + ic
        m_new = jnp.maximum(tot + m, jnp.max(g, axis=1))
        decay = jnp.exp(tot + m - m_new)
        ws = jnp.exp(g - m_new[:, None, :])
        C_new = decay[..., None, None] * C + jnp.einsum("bsh,bshd,bshe->bhde", ws, kc, vc)
        n_new = decay[..., None] * n + jnp.einsum("bsh,bshd->bhd", ws, kc)
        return (C_new, n_new, m_new), h

    init = (C0.astype(jnp.float32), n0.astype(jnp.float32), m0.astype(jnp.float32))
    (C, n, m), h = lax.scan(step, init, xs)
    h = jnp.moveaxis(h, 0, 1).reshape(bsz, seq, N_LSTM_HEADS, LSTM_HEAD_DIM)
    return h, C, n, m


def _merge(att, h, mo, g_out, w_out):
    hl = _rmsnorm(jax.nn.sigmoid(mo) * h.astype(mo.dtype), g_out)
    lead = att.shape[:-1]
    mixed = jnp.concatenate([att.astype(hl.dtype), hl.reshape(lead + (LSTM_WIDTH,))], axis=-1)
    return mixed @ w_out


def _mix_prompt(x, rows, g_mix, w_in, b_i, b_f, g_q, g_k, sinks, g_out, w_out):
    bsz, seq = x.shape[:2]
    nb = seq // CHUNK
    q, k, v, mq, mk, mv, mo, li, lf = _in_proj(x, g_mix, w_in, b_i, b_f)
    q = _rmsnorm(q, g_q)
    k = _rmsnorm(k, g_k)

    def band(t):
        t = t.reshape(bsz, nb, CHUNK, N_KV_HEADS, HEAD_DIM)
        t = jnp.concatenate([jnp.zeros((bsz, WINDOW_CHUNKS) + t.shape[2:], t.dtype), t], axis=1)
        return jnp.concatenate([t[:, w:w + nb] for w in range(WINDOW_CHUNKS + 1)], axis=2)

    n_keys = (WINDOW_CHUNKS + 1) * CHUNK
    qi = jnp.arange(CHUNK)[:, None]
    kj = jnp.arange(n_keys)[None, :]
    dist = jnp.abs(WINDOW + qi - kj)
    valid = (jnp.arange(nb)[:, None] - WINDOW_CHUNKS) * CHUNK + kj >= 0
    att = _sink_attention(q.reshape(bsz, nb, CHUNK, N_KV_HEADS, GQA_GROUP, HEAD_DIM),
                          band(k), band(v), dist, valid, sinks)
    C0 = jnp.zeros((bsz, N_LSTM_HEADS, LSTM_HEAD_DIM, LSTM_HEAD_DIM), jnp.float32)
    n0 = jnp.zeros((bsz, N_LSTM_HEADS, LSTM_HEAD_DIM), jnp.float32)
    m0 = jnp.zeros((bsz, N_LSTM_HEADS), jnp.float32)
    h, C, n, m = _mlstm_scan(mq, mk, mv, li, lf, C0, n0, m0, CHUNK)
    y = _merge(att.reshape(bsz, seq, ATT_WIDTH), h, mo, g_out, w_out)
    return y, k[:, -rows:], v[:, -rows:], C, n, m


def _mix_sample(x, ck, cv, C0, n0, m0, g_mix, w_in, b_i, b_f, g_q, g_k, sinks, g_out, w_out):
    bsz, t = x.shape[:2]
    rows = ck.shape[1]
    q, k, v, mq, mk, mv, mo, li, lf = _in_proj(x, g_mix, w_in, b_i, b_f)
    q = _rmsnorm(q, g_q)
    k = _rmsnorm(k, g_k)
    kk = jnp.concatenate([ck.astype(k.dtype), k], axis=1)
    vv = jnp.concatenate([cv.astype(v.dtype), v], axis=1)
    dist = jnp.abs(rows + jnp.arange(t)[:, None] - jnp.arange(rows + t)[None, :])
    valid = jnp.ones((1, rows + t), dtype=bool)
    att = _sink_attention(q.reshape(bsz, 1, t, N_KV_HEADS, GQA_GROUP, HEAD_DIM),
                          kk[:, None], vv[:, None], dist, valid, sinks)
    h, C, n, m = _mlstm_scan(mq, mk, mv, li, lf, C0, n0, m0, t)
    y = _merge(att.reshape(bsz, t, ATT_WIDTH), h, mo, g_out, w_out)
    return y, kk[:, -rows:], vv[:, -rows:], C, n, m


def setup_inputs(seed: int = 0) -> dict:
    key = jax.random.key(seed)
    ks = jax.random.split(key, 32)
    f32 = jnp.float32
    rows = min(WINDOW, PAST_LEN)

    def nrm(k, shape, scale):
        return jax.random.normal(k, shape, f32) * scale

    def gain(k, shape):
        return 1.0 + 0.05 * jax.random.normal(k, shape, f32)

    return {
        "x_prompt": nrm(ks[0], (BATCH, SEQ, D_MODEL), 1.0),
        "x_sample": nrm(ks[1], (DEC_BATCH, DEC_SEQ, D_MODEL), 1.0),
        "cache_k": nrm(ks[2], (DEPTH, DEC_BATCH, rows, N_KV_HEADS, HEAD_DIM), 1.0),
        "cache_v": nrm(ks[3], (DEPTH, DEC_BATCH, rows, N_KV_HEADS, HEAD_DIM), 1.0),
        "state_C": nrm(ks[4], (DEPTH, DEC_BATCH, N_LSTM_HEADS, LSTM_HEAD_DIM, LSTM_HEAD_DIM), 0.3),
        "state_n": nrm(ks[5], (DEPTH, DEC_BATCH, N_LSTM_HEADS, LSTM_HEAD_DIM), 0.3),
        "state_m": jax.random.uniform(ks[6], (DEPTH, DEC_BATCH, N_LSTM_HEADS), f32, 0.0, 2.0),
        "g_ffn1": gain(ks[7], (DEPTH, D_MODEL)),
        "w1_gate": nrm(ks[8], (DEPTH, D_MODEL, D_FF), D_MODEL ** -0.5),
        "w1_up": nrm(ks[9], (DEPTH, D_MODEL, D_FF), D_MODEL ** -0.5),
        "w1_down": nrm(ks[10], (DEPTH, D_FF, D_MODEL), D_FF ** -0.5),
        "g_mix": gain(ks[11], (DEPTH, D_MODEL)),
        "w_in": nrm(ks[12], (DEPTH, D_MODEL, D_IN_PROJ), D_MODEL ** -0.5),
        "b_igate": nrm(ks[13], (DEPTH, N_LSTM_HEADS), 0.1),
        "b_fgate": jnp.broadcast_to(jnp.linspace(3.0, 6.0, N_LSTM_HEADS, dtype=f32), (DEPTH, N_LSTM_HEADS))
                   + nrm(ks[14], (DEPTH, N_LSTM_HEADS), 0.1),
        "g_q": gain(ks[15], (DEPTH, HEAD_DIM)),
        "g_k": gain(ks[16], (DEPTH, HEAD_DIM)),
        "attn_sinks": nrm(ks[17], (DEPTH, N_ATT_HEADS), 0.5),
        "g_lstm_out": gain(ks[18], (DEPTH, N_LSTM_HEADS, LSTM_HEAD_DIM)),
        "w_out": nrm(ks[19], (DEPTH, D_MIX, D_MODEL), D_MIX ** -0.5),
        "g_ffn2": gain(ks[20], (DEPTH, D_MODEL)),
        "w2_gate": nrm(ks[21], (DEPTH, D_MODEL, D_FF), D_MODEL ** -0.5),
        "w2_up": nrm(ks[22], (DEPTH, D_MODEL, D_FF), D_MODEL ** -0.5),
        "w2_down": nrm(ks[23], (DEPTH, D_FF, D_MODEL), D_FF ** -0.5),
        "g_final": gain(ks[24], (DEPTH, D_MODEL)),
    }


def reference(x_prompt, x_sample, cache_k, cache_v, state_C, state_n, state_m,
              g_ffn1, w1_gate, w1_up, w1_down, g_mix, w_in, b_igate, b_fgate,
              g_q, g_k, attn_sinks, g_lstm_out, w_out, g_ffn2, w2_gate, w2_up, w2_down, g_final):
    rows = cache_k.shape[2]
    xp, xs = x_prompt, x_sample
    kp_l, vp_l, Cp_l, np_l, mp_l = [], [], [], [], []
    ks_l, vs_l, Cs_l, ns_l, ms_l = [], [], [], [], []
    for l in range(DEPTH):
        mix_w = (g_mix[l], w_in[l], b_igate[l], b_fgate[l], g_q[l], g_k[l],
                 attn_sinks[l], g_lstm_out[l], w_out[l])
        xp = xp + _swiglu_half(xp, g_ffn1[l], w1_gate[l], w1_up[l], w1_down[l])
        dp, kp, vp, Cp, n_p, mp = _mix_prompt(xp, rows, *mix_w)
        xp = xp + dp
        xp = xp + _swiglu_half(xp, g_ffn2[l], w2_gate[l], w2_up[l], w2_down[l])
        xp = _rmsnorm(xp, g_final[l])
        xs = xs + _swiglu_half(xs, g_ffn1[l], w1_gate[l], w1_up[l], w1_down[l])
        ds, k_s, v_s, Cs, n_s, m_s = _mix_sample(xs, cache_k[l], cache_v[l], state_C[l],
                                                 state_n[l], state_m[l], *mix_w)
        xs = xs + ds
        xs = xs + _swiglu_half(xs, g_ffn2[l], w2_gate[l], w2_up[l], w2_down[l])
        xs = _rmsnorm(xs, g_final[l])
        kp_l.append(kp.astype(cache_k.dtype)); vp_l.append(vp.astype(cache_v.dtype))
        Cp_l.append(Cp.astype(state_C.dtype)); np_l.append(n_p.astype(state_n.dtype))
        mp_l.append(mp.astype(state_m.dtype))
        ks_l.append(k_s.astype(cache_k.dtype)); vs_l.append(v_s.astype(cache_v.dtype))
        Cs_l.append(Cs.astype(state_C.dtype)); ns_l.append(n_s.astype(state_n.dtype))
        ms_l.append(m_s.astype(state_m.dtype))
    k_prompt = jnp.stack(kp_l, 0)
    v_prompt = jnp.stack(vp_l, 0)
    C_prompt = jnp.stack(Cp_l, 0)
    n_prompt = jnp.stack(np_l, 0)
    m_prompt = jnp.stack(mp_l, 0)
    k_sample = jnp.stack(ks_l, 0)
    v_sample = jnp.stack(vs_l, 0)
    C_sample = jnp.stack(Cs_l, 0)
    n_sample = jnp.stack(ns_l, 0)
    m_sample = jnp.stack(ms_l, 0)
    return (xp, xs, k_prompt, v_prompt, C_prompt, n_prompt, m_prompt,
            k_sample, v_sample, C_sample, n_sample, m_sample)
```

```python
import dataclasses
import functools

import jax
import jax.numpy as jnp
from jax import lax
from jax.experimental import pallas as pl
from jax.experimental.pallas import tpu as pltpu

F32 = jnp.float32
BF16 = jnp.bfloat16

D_MODEL = 1024
D_FF = 2816
CHUNK = 64
WINDOW = 128
N_KEYS = WINDOW + CHUNK
N_ATT_HEADS = 8
HEAD_DIM = 64
N_KV_HEADS = 2
GQA_GROUP = N_ATT_HEADS // N_KV_HEADS
ATT_WIDTH = N_ATT_HEADS * HEAD_DIM
KV_WIDTH = N_KV_HEADS * HEAD_DIM
N_LSTM_HEADS = 4
LSTM_HEAD_DIM = 128
LSTM_WIDTH = N_LSTM_HEADS * LSTM_HEAD_DIM
EPS = 1e-6
LANES = 128
GATE_LANE0 = 4

_OFF_Q = 0
_OFF_K = _OFF_Q + ATT_WIDTH
_OFF_V = _OFF_K + KV_WIDTH
_OFF_MQ = _OFF_V + KV_WIDTH
_OFF_MK = _OFF_MQ + LSTM_WIDTH
_OFF_MV = _OFF_MK + LSTM_WIDTH
_OFF_MO = _OFF_MV + LSTM_WIDTH
_OFF_GATE = _OFF_MO + LSTM_WIDTH
_W_IN_PADDED = _OFF_GATE + LANES

_V7X_VMEM_BYTES = 64 * 1024 * 1024
_NT = (((1,), (1,)), ((), ()))
_TN = (((0,), (0,)), ((), ()))


@dataclasses.dataclass(frozen=True)
class _Tile:
    bb: int
    ts: int
    n_seq: int
    has_history: bool


def _dot(a, b):
    return jnp.dot(a, b, preferred_element_type=F32)


def _rms(x, g):
    ms = jnp.mean(x * x, axis=-1, keepdims=True)
    return x * lax.rsqrt(ms + EPS) * g


def _rms_head_pairs(x, g, lo):
    outs = []
    for j in range(x.shape[-1] // LANES):
        t = x[:, j * LANES:(j + 1) * LANES]
        y = t * t
        s_lo = jnp.sum(jnp.where(lo, y, 0.0), axis=-1, keepdims=True)
        s_hi = jnp.sum(jnp.where(lo, 0.0, y), axis=-1, keepdims=True)
        inv = jnp.where(lo, lax.rsqrt(s_lo * (1.0 / HEAD_DIM) + EPS),
                        lax.rsqrt(s_hi * (1.0 / HEAD_DIM) + EPS))
        outs.append(t * inv * g[:, j * LANES:(j + 1) * LANES])
    return jnp.concatenate(outs, axis=-1)


def _ffn_half(x, g_ref, wg_ref, wu_ref, wd_ref):
    h = _rms(x, g_ref[...]).astype(BF16)
    gate = _dot(h, wg_ref[...])
    up = _dot(h, wu_ref[...])
    act = (gate * jax.nn.sigmoid(gate) * up).astype(BF16)
    return x + 0.5 * _dot(act, wd_ref[...])


def _kv_variants(t, lo):
    r = pltpu.roll(t, HEAD_DIM, axis=1)
    z = jnp.zeros_like(t)
    return (jnp.where(lo, t, z).astype(BF16), jnp.where(lo, z, r).astype(BF16),
            jnp.where(lo, r, z).astype(BF16), jnp.where(lo, z, t).astype(BF16))


def _cumsum_chunks(x, tri):
    hi = x.astype(BF16)
    r1 = x - hi.astype(F32)
    mid = r1.astype(BF16)
    lo = (r1 - mid.astype(F32)).astype(BF16)
    return _dot(tri, hi) + _dot(tri, mid) + _dot(tri, lo)


def _layer_kernel(cfg, sink_ref, x_ref, ck_ref, cv_ref, c0_ref, n0_ref, m0_ref,
                  g1_ref, w1g_ref, w1u_ref, w1d_ref, gmix_ref, win_ref, gbias_ref,
                  gq_ref, gk_ref, glo_ref, wo_ref, g2_ref, w2g_ref, w2u_ref, w2d_ref,
                  gfin_ref,
                  y_ref, ko_ref, vo_ref, c_ref, n_ref, m_ref,
                  kh_ref, vh_ref):
    bb, ts = cfg.bb, cfg.ts
    rows = bb * ts
    n_chunks = ts // CHUNK
    s_idx = pl.program_id(1)
    neg_inf = F32(-jnp.inf)

    lane = lax.broadcasted_iota(jnp.int32, (1, LANES), 1)
    lo = lane < HEAD_DIM

    x = x_ref[...].reshape(rows, D_MODEL)
    x1 = _ffn_half(x, g1_ref, w1g_ref, w1u_ref, w1d_ref)
    h2 = _rms(x1, gmix_ref[...]).astype(BF16)

    def proj(off, width):
        return _dot(h2, win_ref[:, off:off + width])

    q = _rms_head_pairs(proj(_OFF_Q, ATT_WIDTH), gq_ref[...], lo)
    qb = (q * (HEAD_DIM ** -0.5)).astype(BF16)
    kn = _rms_head_pairs(proj(_OFF_K, KV_WIDTH), gk_ref[...], lo)
    v = proj(_OFF_V, KV_WIDTH)
    mq = proj(_OFF_MQ, LSTM_WIDTH)
    mqb = mq.astype(BF16)
    mk = proj(_OFF_MK, LSTM_WIDTH) * (LSTM_HEAD_DIM ** -0.5)
    mkb = mk.astype(BF16)
    mvb = proj(_OFF_MV, LSTM_WIDTH).astype(BF16)
    mo = proj(_OFF_MO, LSTM_WIDTH)
    gates = proj(_OFF_GATE, LANES) + gbias_ref[...]

    @pl.when(s_idx == 0)
    def _():
        for b in range(bb):
            if cfg.has_history:
                kvar = _kv_variants(ck_ref[b], lo)
                vvar = _kv_variants(cv_ref[b], lo)
                for i in range(4):
                    kh_ref[i, b, 0:WINDOW, :] = kvar[i]
                    vh_ref[i, b, 0:WINDOW, :] = vvar[i]
            else:
                zero = jnp.zeros((WINDOW, LANES), BF16)
                for i in range(4):
                    kh_ref[i, b, 0:WINDOW, :] = zero
                    vh_ref[i, b, 0:WINDOW, :] = zero
            c_ref[b] = c0_ref[b]
            n_ref[b] = n0_ref[b]
            m_ref[b] = m0_ref[b]

    log_f = jnp.minimum(gates, 0.0) - jnp.log1p(jnp.exp(-jnp.abs(gates)))
    log_i = pltpu.roll(gates, GATE_LANE0, axis=1)
    ri = lax.broadcasted_iota(jnp.int32, (rows, rows), 0)
    ci = lax.broadcasted_iota(jnp.int32, (rows, rows), 1)
    tri = ((ci <= ri) & ((ci // CHUNK) == (ri // CHUNK))).astype(BF16)
    b_all = _cumsum_chunks(log_f, tri)
    r_all = log_i - b_all
    r_rows = r_all.T

    qi = lax.broadcasted_iota(jnp.int32, (CHUNK, N_KEYS), 0)
    kj = lax.broadcasted_iota(jnp.int32, (CHUNK, N_KEYS), 1)
    dist = jnp.abs(WINDOW + qi - kj).astype(F32)
    half = lax.broadcasted_iota(jnp.int32, (2 * CHUNK, 1), 0) < CHUNK
    alibi, sink_col = [], []
    for kh in range(N_KV_HEADS):
        for pos in range(2):
            h_a = GQA_GROUP * kh + pos
            h_b = h_a + 2
            alibi.append(jnp.concatenate(
                [dist * (2.0 ** -(h_a + 1)), dist * (2.0 ** -(h_b + 1))], axis=0))
            sink_col.append(jnp.where(half, sink_ref[h_a], sink_ref[h_b]))
    key_idx = lax.broadcasted_iota(jnp.int32, (1, N_KEYS), 1)
    ti = lax.broadcasted_iota(jnp.int32, (CHUNK, CHUNK), 0)
    si = lax.broadcasted_iota(jnp.int32, (CHUNK, CHUNK), 1)
    causal = si <= ti

    att_rows, hl_rows = [], []
    for b in range(bb):
        r0 = b * ts
        kvar = _kv_variants(kn[r0:r0 + ts], lo)
        vvar = _kv_variants(v[r0:r0 + ts], lo)
        for i in range(4):
            kh_ref[i, b, WINDOW:WINDOW + ts, :] = kvar[i]
            vh_ref[i, b, WINDOW:WINDOW + ts, :] = vvar[i]

        c_state = [c_ref[b, h] for h in range(N_LSTM_HEADS)]
        n_state = [n_ref[b, h:h + 1, :] for h in range(N_LSTM_HEADS)]
        m_row = m_ref[b]

        for c in range(n_chunks):
            rs = r0 + c * CHUNK
            k0 = c * CHUNK

            if cfg.has_history:
                bias = None
            else:
                first_valid = (WINDOW // CHUNK - (s_idx * n_chunks + c)) * CHUNK
                bias = jnp.where(key_idx >= first_valid, 0.0, neg_inf)
            pair_out = [None] * 4
            for kh in range(N_KV_HEADS):
                qs = jnp.concatenate(
                    [qb[rs:rs + CHUNK, (2 * kh) * LANES:(2 * kh + 1) * LANES],
                     qb[rs:rs + CHUNK, (2 * kh + 1) * LANES:(2 * kh + 2) * LANES]], axis=0)
                for pos in range(2):
                    i = 2 * kh + pos
                    keys = kh_ref[i, b, k0:k0 + N_KEYS, :]
                    vals = vh_ref[i, b, k0:k0 + N_KEYS, :]
                    sc = lax.dot_general(qs, keys, _NT, preferred_element_type=F32)
                    sc = sc - alibi[i]
                    if bias is not None:
                        sc = sc + bias
                    mx = jnp.maximum(jnp.max(sc, axis=-1, keepdims=True), sink_col[i])
                    p = jnp.exp(sc - mx)
                    den = jnp.sum(p, axis=-1, keepdims=True) + jnp.exp(sink_col[i] - mx)
                    o = _dot(p.astype(BF16), vals) / den
                    for part in range(2):
                        j = 2 * kh + part
                        piece = o[part * CHUNK:(part + 1) * CHUNK]
                        pair_out[j] = piece if pair_out[j] is None else pair_out[j] + piece
            att_rows.append(jnp.concatenate(pair_out, axis=-1))

            bc = b_all[rs:rs + CHUNK]
            rc = r_all[rs:rs + CHUNK]
            tot = bc[CHUNK - 1:CHUNK, :]
            inter = bc + m_row
            g = tot + rc
            m_new = jnp.maximum(tot + m_row, jnp.max(g, axis=0, keepdims=True))
            decay = jnp.exp(tot + m_row - m_new)
            ws = jnp.exp(g - m_new)
            heads = []
            for h in range(N_LSTM_HEADS):
                gl = GATE_LANE0 + h
                hs = slice(h * LSTM_HEAD_DIM, (h + 1) * LSTM_HEAD_DIM)
                b_col = bc[:, gl:gl + 1]
                inter_col = inter[:, gl:gl + 1]
                r_row = r_rows[gl:gl + 1, rs:rs + CHUNK]
                d = jnp.where(causal, b_col + r_row, neg_inf)
                mt = jnp.maximum(inter_col, jnp.max(d, axis=1, keepdims=True))
                qh = mqb[rs:rs + CHUNK, hs]
                kh_ = mkb[rs:rs + CHUNK, hs]
                vh_ = mvb[rs:rs + CHUNK, hs]
                qk = lax.dot_general(qh, kh_, _NT, preferred_element_type=F32)
                a = jnp.exp(d - mt) * qk
                w_inter = jnp.exp(inter_col - mt)
                q_c = _dot(qh, c_state[h].astype(BF16))
                q_n = jnp.sum(mq[rs:rs + CHUNK, hs] * n_state[h], axis=1, keepdims=True)
                num = _dot(a.astype(BF16), vh_) + w_inter * q_c
                den = jnp.sum(a, axis=1, keepdims=True) + w_inter * q_n
                heads.append(num / jnp.maximum(jnp.abs(den), jnp.exp(-mt)))
                wk = ws[:, gl:gl + 1] * mk[rs:rs + CHUNK, hs]
                dec = decay[:, gl:gl + 1]
                c_state[h] = dec * c_state[h] + lax.dot_general(
                    wk.astype(BF16), vh_, _TN, preferred_element_type=F32)
                n_state[h] = dec * n_state[h] + jnp.sum(wk, axis=0, keepdims=True)
            m_row = m_new
            hl_rows.append(jnp.concatenate(heads, axis=-1))

        for h in range(N_LSTM_HEADS):
            c_ref[b, h] = c_state[h]
            n_ref[b, h:h + 1, :] = n_state[h]
        m_ref[b] = m_row

        if cfg.n_seq > 1:
            for i in range(4):
                kh_ref[i, b, 0:WINDOW, :] = kh_ref[i, b, ts:ts + WINDOW, :]
                vh_ref[i, b, 0:WINDOW, :] = vh_ref[i, b, ts:ts + WINDOW, :]

        @pl.when(s_idx == cfg.n_seq - 1)
        def _():
            if ts >= WINDOW:
                ko_ref[b] = kn[r0 + ts - WINDOW:r0 + ts]
                vo_ref[b] = v[r0 + ts - WINDOW:r0 + ts]
            else:
                ko_ref[b, 0:WINDOW - ts, :] = ck_ref[b, ts:WINDOW, :]
                vo_ref[b, 0:WINDOW - ts, :] = cv_ref[b, ts:WINDOW, :]
                ko_ref[b, WINDOW - ts:WINDOW, :] = kn[r0:r0 + ts]
                vo_ref[b, WINDOW - ts:WINDOW, :] = v[r0:r0 + ts]

    att = jnp.concatenate(att_rows, axis=0).astype(BF16)
    hm = jnp.concatenate(hl_rows, axis=0)
    gated = jax.nn.sigmoid(mo) * hm
    glo = glo_ref[...]
    hl = jnp.concatenate(
        [_rms(gated[:, h * LSTM_HEAD_DIM:(h + 1) * LSTM_HEAD_DIM],
              glo[:, h * LSTM_HEAD_DIM:(h + 1) * LSTM_HEAD_DIM])
         for h in range(N_LSTM_HEADS)], axis=-1).astype(BF16)
    mixed = jnp.concatenate([att, hl], axis=-1)
    x2 = x1 + _dot(mixed, wo_ref[...])
    x3 = _ffn_half(x2, g2_ref, w2g_ref, w2u_ref, w2d_ref)
    y_ref[...] = _rms(x3, gfin_ref[...]).reshape(bb, ts, D_MODEL)


def _resident(shape):
    return pl.BlockSpec(shape, lambda b, s: (0,) * len(shape), pipeline_mode=pl.Buffered(1))


def _run_layer(cfg, x, ck, cv, c0, n0, m0, sinks, weights):
    batch, seq, _ = x.shape
    assert batch % cfg.bb == 0 and seq == cfg.ts * cfg.n_seq and cfg.ts % CHUNK == 0
    assert cfg.ts >= WINDOW or cfg.n_seq == 1
    bb, ts = cfg.bb, cfg.ts
    per_batch = lambda *tail: pl.BlockSpec((bb,) + tail, lambda b, s: (b,) + (0,) * len(tail))
    in_specs = [
        pl.BlockSpec(memory_space=pltpu.SMEM),
        pl.BlockSpec((bb, ts, D_MODEL), lambda b, s: (b, s, 0)),
        per_batch(WINDOW, KV_WIDTH), per_batch(WINDOW, KV_WIDTH),
        per_batch(N_LSTM_HEADS, LSTM_HEAD_DIM, LSTM_HEAD_DIM),
        per_batch(N_LSTM_HEADS, LSTM_HEAD_DIM), per_batch(1, LANES),
    ] + [_resident(w.shape) for w in weights]
    out_shape = (
        jax.ShapeDtypeStruct((batch, seq, D_MODEL), F32),
        jax.ShapeDtypeStruct((batch, WINDOW, KV_WIDTH), F32),
        jax.ShapeDtypeStruct((batch, WINDOW, KV_WIDTH), F32),
        jax.ShapeDtypeStruct((batch, N_LSTM_HEADS, LSTM_HEAD_DIM, LSTM_HEAD_DIM), F32),
        jax.ShapeDtypeStruct((batch, N_LSTM_HEADS, LSTM_HEAD_DIM), F32),
        jax.ShapeDtypeStruct((batch, 1, LANES), F32),
    )
    out_specs = (
        pl.BlockSpec((bb, ts, D_MODEL), lambda b, s: (b, s, 0)),
        per_batch(WINDOW, KV_WIDTH), per_batch(WINDOW, KV_WIDTH),
        per_batch(N_LSTM_HEADS, LSTM_HEAD_DIM, LSTM_HEAD_DIM),
        per_batch(N_LSTM_HEADS, LSTM_HEAD_DIM), per_batch(1, LANES),
    )
    hist = pltpu.VMEM((4, bb, WINDOW + ts, LANES), BF16)
    return pl.pallas_call(
        functools.partial(_layer_kernel, cfg),
        grid=(batch // bb, cfg.n_seq),
        in_specs=in_specs, out_specs=out_specs, out_shape=out_shape,
        scratch_shapes=[hist, hist],
        compiler_params=pltpu.CompilerParams(
            dimension_semantics=("arbitrary", "arbitrary"),
            vmem_limit_bytes=_V7X_VMEM_BYTES - 2 * 1024 * 1024),
        name=f"hybrid_layer_bb{bb}_ts{ts}",
    )(sinks, x, ck, cv, c0, n0, m0, *weights)


def _pad_gate_lanes(t):
    out = jnp.zeros((t.shape[0], 1, LANES), F32)
    return out.at[:, 0, GATE_LANE0:GATE_LANE0 + N_LSTM_HEADS].set(t)


def kernel(x_prompt, x_sample, cache_k, cache_v, state_C, state_n, state_m, g_ffn1, w1_gate, w1_up, w1_down, g_mix, w_in, b_igate, b_fgate, g_q, g_k, attn_sinks, g_lstm_out, w_out, g_ffn2, w2_gate, w2_up, w2_down, g_final):
    depth = w_in.shape[0]
    assert depth == 1
    l = 0
    batch = x_prompt.shape[0]
    dec_batch = x_sample.shape[0]
    row = lambda t: t.reshape(1, -1).astype(F32)
    gate_pad = jnp.zeros((D_MODEL, LANES - 2 * N_LSTM_HEADS), F32)
    win = jnp.concatenate([w_in[l], gate_pad], axis=1).astype(BF16)
    gbias = jnp.concatenate(
        [b_igate[l], b_fgate[l], jnp.zeros((LANES - 2 * N_LSTM_HEADS,), F32)]).reshape(1, LANES)
    weights = (
        row(g_ffn1[l]), w1_gate[l].astype(BF16), w1_up[l].astype(BF16), w1_down[l].astype(BF16),
        row(g_mix[l]), win, gbias,
        row(jnp.tile(g_q[l], N_ATT_HEADS)), row(jnp.tile(g_k[l], N_KV_HEADS)),
        row(g_lstm_out[l]), w_out[l].astype(BF16),
        row(g_ffn2[l]), w2_gate[l].astype(BF16), w2_up[l].astype(BF16), w2_down[l].astype(BF16),
        row(g_final[l]),
    )
    sinks = attn_sinks[l].astype(F32)

    zeros_kv = jnp.zeros((batch, WINDOW, KV_WIDTH), F32)
    prompt_cfg = _Tile(bb=1, ts=256, n_seq=x_prompt.shape[1] // 256, has_history=False)
    yp, kp, vp, cp, n_p, mp = _run_layer(
        prompt_cfg, x_prompt, zeros_kv, zeros_kv,
        jnp.zeros((batch, N_LSTM_HEADS, LSTM_HEAD_DIM, LSTM_HEAD_DIM), F32),
        jnp.zeros((batch, N_LSTM_HEADS, LSTM_HEAD_DIM), F32),
        jnp.zeros((batch, 1, LANES), F32), sinks, weights)

    rows = cache_k.shape[2]
    assert rows == WINDOW
    sample_cfg = _Tile(bb=4, ts=x_sample.shape[1], n_seq=1, has_history=True)
    ys, ks, vs, cs, n_s, ms = _run_layer(
        sample_cfg, x_sample,
        cache_k[l].reshape(dec_batch, rows, KV_WIDTH), cache_v[l].reshape(dec_batch, rows, KV_WIDTH),
        state_C[l], state_n[l], _pad_gate_lanes(state_m[l]), sinks, weights)

    kv_shape = lambda b: (1, b, rows, N_KV_HEADS, HEAD_DIM)
    m_out = lambda t: t[:, 0, GATE_LANE0:GATE_LANE0 + N_LSTM_HEADS][None]
    return (yp, ys,
            kp.reshape(kv_shape(batch)), vp.reshape(kv_shape(batch)),
            cp[None], n_p[None], m_out(mp),
            ks.reshape(kv_shape(dec_batch)), vs.reshape(kv_shape(dec_batch)),
            cs[None], n_s[None], m_out(ms))
```

```python
import dataclasses
import functools

import jax
import jax.numpy as jnp
from jax import lax
from jax.experimental import pallas as pl
from jax.experimental.pallas import tpu as pltpu

F32 = jnp.float32
BF16 = jnp.bfloat16

D_MODEL = 1024
D_FF = 2816
CHUNK = 64
WINDOW = 128
N_KEYS = WINDOW + CHUNK
N_ATT_HEADS = 8
HEAD_DIM = 64
N_KV_HEADS = 2
GQA_GROUP = N_ATT_HEADS // N_KV_HEADS
ATT_WIDTH = N_ATT_HEADS * HEAD_DIM
KV_WIDTH = N_KV_HEADS * HEAD_DIM
N_LSTM_HEADS = 4
LSTM_HEAD_DIM = 128
LSTM_WIDTH = N_LSTM_HEADS * LSTM_HEAD_DIM
EPS = 1e-6
LANES = 128
GATE_LANE0 = 4

_OFF_Q = 0
_OFF_K = _OFF_Q + ATT_WIDTH
_OFF_V = _OFF_K + KV_WIDTH
_OFF_MQ = _OFF_V + KV_WIDTH
_OFF_MK = _OFF_MQ + LSTM_WIDTH
_OFF_MV = _OFF_MK + LSTM_WIDTH
_OFF_MO = _OFF_MV + LSTM_WIDTH
_OFF_GATE = _OFF_MO + LSTM_WIDTH
_W_IN_PADDED = _OFF_GATE + LANES

_V7X_VMEM_BYTES = 64 * 1024 * 1024
_VMEM_LIMIT = _V7X_VMEM_BYTES - 2 * 1024 * 1024
_FFN_ROWS = 512
_NT = (((1,), (1,)), ((), ()))
_TN = (((0,), (0,)), ((), ()))


@dataclasses.dataclass(frozen=True)
class _Tile:
    bb: int
    ts: int
    n_seq: int
    n_tiles: int
    has_history: bool


def _dot(a, b):
    return jnp.dot(a, b, preferred_element_type=F32)


def _rms(x, g):
    ms = jnp.mean(x * x, axis=-1, keepdims=True)
    return x * lax.rsqrt(ms + EPS) * g


def _rms_head_pairs(x, g, lo):
    outs = []
    for j in range(x.shape[-1] // LANES):
        t = x[:, j * LANES:(j + 1) * LANES]
        y = t * t
        s_lo = jnp.sum(jnp.where(lo, y, 0.0), axis=-1, keepdims=True)
        s_hi = jnp.sum(jnp.where(lo, 0.0, y), axis=-1, keepdims=True)
        inv = jnp.where(lo, lax.rsqrt(s_lo * (1.0 / HEAD_DIM) + EPS),
                        lax.rsqrt(s_hi * (1.0 / HEAD_DIM) + EPS))
        outs.append(t * inv * g[:, j * LANES:(j + 1) * LANES])
    return jnp.concatenate(outs, axis=-1)


def _lane_column(x, sel):
    return jnp.sum(jnp.where(sel, x, 0.0), axis=1, keepdims=True)


def _ffn_half(x, g_ref, wg_ref, wu_ref, wd_ref):
    h = _rms(x, g_ref[...]).astype(BF16)
    gate = _dot(h, wg_ref[...])
    up = _dot(h, wu_ref[...])
    act = (gate * jax.nn.sigmoid(gate) * up).astype(BF16)
    return x + 0.5 * _dot(act, wd_ref[...])


def _kv_variants(t, lo):
    r = pltpu.roll(t, HEAD_DIM, axis=1)
    z = jnp.zeros_like(t)
    return (jnp.where(lo, t, z).astype(BF16), jnp.where(lo, z, r).astype(BF16),
            jnp.where(lo, r, z).astype(BF16), jnp.where(lo, z, t).astype(BF16))


def _cumsum_chunks(x, tri):
    hi = x.astype(BF16)
    r1 = x - hi.astype(F32)
    mid = r1.astype(BF16)
    lo = (r1 - mid.astype(F32)).astype(BF16)
    return _dot(tri, hi) + _dot(tri, mid) + _dot(tri, lo)


def _mixer_kernel(cfg, *refs):
    it = iter(refs)
    sink_ref, x_ref = next(it), next(it)
    if cfg.has_history:
        ck_ref, cv_ref, c0_ref, n0_ref, m0_ref = (next(it) for _ in range(5))
    (g1_ref, w1g_ref, w1u_ref, w1d_ref, gmix_ref, win_ref, gbias_ref,
     gq_ref, gk_ref, glo_ref, wo_ref) = (next(it) for _ in range(11))
    x2_ref, ko_ref, vo_ref, c_ref, n_ref, m_ref = (next(it) for _ in range(6))
    kh_ref, vh_ref, x1s_ref, zs_ref = (next(it) for _ in range(4))

    bb, ts = cfg.bb, cfg.ts
    rows = bb * ts
    n_chunks = ts // CHUNK
    step = pl.program_id(0)
    slot_a = lax.rem(step, 2)
    slot_b = 1 - slot_a
    s_idx = lax.rem(jnp.maximum(step - 1, 0), cfg.n_seq)
    neg_inf = F32(-jnp.inf)

    lane = lax.broadcasted_iota(jnp.int32, (1, LANES), 1)
    lo = lane < HEAD_DIM

    @pl.when(step == 0)
    def _():
        x1s_ref[1] = jnp.zeros((rows, D_MODEL), F32)
        zs_ref[1] = jnp.zeros((rows, _W_IN_PADDED), F32)

    @pl.when(s_idx == 0)
    def _():
        for b in range(bb):
            if cfg.has_history:
                kvar = _kv_variants(ck_ref[b], lo)
                vvar = _kv_variants(cv_ref[b], lo)
                for i in range(4):
                    kh_ref[i, b, 0:WINDOW, :] = kvar[i]
                    vh_ref[i, b, 0:WINDOW, :] = vvar[i]
                ko_ref[b] = ck_ref[b]
                vo_ref[b] = cv_ref[b]
                c_ref[b] = c0_ref[b]
                n_ref[b] = n0_ref[b]
                m_ref[b] = m0_ref[b]
            else:
                zero = jnp.zeros((WINDOW, LANES), BF16)
                for i in range(4):
                    kh_ref[i, b, 0:WINDOW, :] = zero
                    vh_ref[i, b, 0:WINDOW, :] = zero
                ko_ref[b] = jnp.zeros((WINDOW, KV_WIDTH), F32)
                vo_ref[b] = jnp.zeros((WINDOW, KV_WIDTH), F32)
                c_ref[b] = jnp.zeros((N_LSTM_HEADS, LSTM_HEAD_DIM, LSTM_HEAD_DIM), F32)
                n_ref[b] = jnp.zeros((N_LSTM_HEADS, LSTM_HEAD_DIM), F32)
                m_ref[b] = jnp.zeros((1, LANES), F32)

    def proj(off, width):
        return zs_ref[slot_b, :, off:off + width]

    x1 = x1s_ref[slot_b]
    q = _rms_head_pairs(proj(_OFF_Q, ATT_WIDTH), gq_ref[...], lo)
    qb = (q * (HEAD_DIM ** -0.5)).astype(BF16)
    kn = _rms_head_pairs(proj(_OFF_K, KV_WIDTH), gk_ref[...], lo)
    v = proj(_OFF_V, KV_WIDTH)
    mq = proj(_OFF_MQ, LSTM_WIDTH)
    mqb = mq.astype(BF16)
    mk = proj(_OFF_MK, LSTM_WIDTH) * (LSTM_HEAD_DIM ** -0.5)
    mkb = mk.astype(BF16)
    mvb = proj(_OFF_MV, LSTM_WIDTH).astype(BF16)
    mo = proj(_OFF_MO, LSTM_WIDTH)
    gates = proj(_OFF_GATE, LANES) + gbias_ref[...]

    log_f = jnp.minimum(gates, 0.0) - jnp.log1p(jnp.exp(-jnp.abs(gates)))
    log_i = pltpu.roll(gates, GATE_LANE0, axis=1)
    ri = lax.broadcasted_iota(jnp.int32, (rows, rows), 0)
    ci = lax.broadcasted_iota(jnp.int32, (rows, rows), 1)
    tri = ((ci <= ri) & ((ci // CHUNK) == (ri // CHUNK))).astype(BF16)
    b_all = _cumsum_chunks(log_f, tri)
    r_all = log_i - b_all
    r_rows = r_all.T

    qi = lax.broadcasted_iota(jnp.int32, (CHUNK, N_KEYS), 0)
    kj = lax.broadcasted_iota(jnp.int32, (CHUNK, N_KEYS), 1)
    dist = jnp.abs(WINDOW + qi - kj).astype(F32)
    half = lax.broadcasted_iota(jnp.int32, (2 * CHUNK, 1), 0) < CHUNK
    alibi, sink_col = [], []
    for kh in range(N_KV_HEADS):
        for pos in range(2):
            h_a = GQA_GROUP * kh + pos
            h_b = h_a + 2
            alibi.append(jnp.concatenate(
                [dist * (2.0 ** -(h_a + 1)), dist * (2.0 ** -(h_b + 1))], axis=0))
            sink_col.append(jnp.where(half, sink_ref[h_a], sink_ref[h_b]))
    key_idx = lax.broadcasted_iota(jnp.int32, (1, N_KEYS), 1)
    ti = lax.broadcasted_iota(jnp.int32, (CHUNK, CHUNK), 0)
    si = lax.broadcasted_iota(jnp.int32, (CHUNK, CHUNK), 1)
    causal = si <= ti

    assert n_chunks == 1, "stage B is written phase-major for one chunk per batch row"
    row0 = [b * ts for b in range(bb)]

    for b in range(bb):
        kvar = _kv_variants(kn[row0[b]:row0[b] + ts], lo)
        vvar = _kv_variants(v[row0[b]:row0[b] + ts], lo)
        for i in range(4):
            kh_ref[i, b, WINDOW:WINDOW + ts, :] = kvar[i]
            vh_ref[i, b, WINDOW:WINDOW + ts, :] = vvar[i]
    if cfg.has_history:
        bias = None
    else:
        first_valid = (WINDOW // CHUNK - s_idx) * CHUNK
        bias = jnp.where(key_idx >= first_valid, 0.0, neg_inf)
    att_jobs = [(b, kh, pos) for b in range(bb) for kh in range(N_KV_HEADS) for pos in range(2)]
    scores = []
    for b, kh, pos in att_jobs:
        rs = row0[b]
        qs = jnp.concatenate(
            [qb[rs:rs + CHUNK, (2 * kh) * LANES:(2 * kh + 1) * LANES],
             qb[rs:rs + CHUNK, (2 * kh + 1) * LANES:(2 * kh + 2) * LANES]], axis=0)
        keys = kh_ref[2 * kh + pos, b, 0:N_KEYS, :]
        scores.append(lax.dot_general(qs, keys, _NT, preferred_element_type=F32))
    probs, dens = [], []
    for (b, kh, pos), sc in zip(att_jobs, scores):
        i = 2 * kh + pos
        sc = sc - alibi[i]
        if bias is not None:
            sc = sc + bias
        mx = jnp.maximum(jnp.max(sc, axis=-1, keepdims=True), sink_col[i])
        p = jnp.exp(sc - mx)
        dens.append(jnp.sum(p, axis=-1, keepdims=True) + jnp.exp(sink_col[i] - mx))
        probs.append(p.astype(BF16))
    pair_out = [[None] * 4 for _ in range(bb)]
    for (b, kh, pos), p, den in zip(att_jobs, probs, dens):
        vals = vh_ref[2 * kh + pos, b, 0:N_KEYS, :]
        o = _dot(p, vals) / den
        for part in range(2):
            j = 2 * kh + part
            piece = o[part * CHUNK:(part + 1) * CHUNK]
            pair_out[b][j] = piece if pair_out[b][j] is None else pair_out[b][j] + piece
    att_rows = [jnp.concatenate(pair_out[b], axis=-1) for b in range(bb)]

    m_rows = [m_ref[b] for b in range(bb)]
    bcs = [b_all[row0[b]:row0[b] + CHUNK] for b in range(bb)]
    rcs = [r_all[row0[b]:row0[b] + CHUNK] for b in range(bb)]
    inters, decays, wss, m_news = [], [], [], []
    for b in range(bb):
        tot = bcs[b][CHUNK - 1:CHUNK, :]
        g = tot + rcs[b]
        m_new = jnp.maximum(tot + m_rows[b], jnp.max(g, axis=0, keepdims=True))
        inters.append(bcs[b] + m_rows[b])
        decays.append(jnp.exp(tot + m_rows[b] - m_new))
        wss.append(jnp.exp(g - m_new))
        m_news.append(m_new)
    jobs = [(b, h) for b in range(bb) for h in range(N_LSTM_HEADS)]
    hsl = lambda h: slice(h * LSTM_HEAD_DIM, (h + 1) * LSTM_HEAD_DIM)
    rsl = lambda b: slice(row0[b], row0[b] + CHUNK)
    c_old = {j: c_ref[j[0], j[1]] for j in jobs}
    n_old = {j: n_ref[j[0], j[1]:j[1] + 1, :] for j in jobs}
    qk = {(b, h): lax.dot_general(mqb[rsl(b), hsl(h)], mkb[rsl(b), hsl(h)], _NT,
                                  preferred_element_type=F32) for b, h in jobs}
    q_c = {(b, h): _dot(mqb[rsl(b), hsl(h)], c_old[(b, h)].astype(BF16)) for b, h in jobs}
    a_mat, mts, w_inters = {}, {}, {}
    for b, h in jobs:
        sel = lane == GATE_LANE0 + h
        b_col = _lane_column(bcs[b], sel)
        inter_col = _lane_column(inters[b], sel)
        r_row = r_rows[GATE_LANE0 + h:GATE_LANE0 + h + 1, rsl(b)]
        d = jnp.where(causal, b_col + r_row, neg_inf)
        mt = jnp.maximum(inter_col, jnp.max(d, axis=1, keepdims=True))
        a_mat[(b, h)] = jnp.exp(d - mt) * qk[(b, h)]
        mts[(b, h)] = mt
        w_inters[(b, h)] = jnp.exp(inter_col - mt)
    intra = {(b, h): _dot(a_mat[(b, h)].astype(BF16), mvb[rsl(b), hsl(h)]) for b, h in jobs}
    head_out = {}
    for b, h in jobs:
        j = (b, h)
        q_n = jnp.sum(mq[rsl(b), hsl(h)] * n_old[j], axis=1, keepdims=True)
        num = intra[j] + w_inters[j] * q_c[j]
        den = jnp.sum(a_mat[j], axis=1, keepdims=True) + w_inters[j] * q_n
        head_out[j] = num / jnp.maximum(jnp.abs(den), jnp.exp(-mts[j]))
    wks = {}
    for b, h in jobs:
        sel = lane == GATE_LANE0 + h
        wks[(b, h)] = _lane_column(wss[b], sel) * mk[rsl(b), hsl(h)]
    c_upd = {(b, h): lax.dot_general(wks[(b, h)].astype(BF16), mvb[rsl(b), hsl(h)], _TN,
                                     preferred_element_type=F32) for b, h in jobs}
    for b, h in jobs:
        j = (b, h)
        dec = _lane_column(decays[b], lane == GATE_LANE0 + h)
        c_ref[b, h] = dec * c_old[j] + c_upd[j]
        n_ref[b, h:h + 1, :] = dec * n_old[j] + jnp.sum(wks[j], axis=0, keepdims=True)
    hl_rows = [jnp.concatenate([head_out[(b, h)] for h in range(N_LSTM_HEADS)], axis=-1)
               for b in range(bb)]

    for b in range(bb):
        r0 = row0[b]
        m_ref[b] = m_news[b]
        if cfg.n_seq > 1:
            for i in range(4):
                kh_ref[i, b, 0:WINDOW, :] = kh_ref[i, b, ts:ts + WINDOW, :]
                vh_ref[i, b, 0:WINDOW, :] = vh_ref[i, b, ts:ts + WINDOW, :]
        ko_ref[b, 0:WINDOW - ts, :] = ko_ref[b, ts:WINDOW, :]
        vo_ref[b, 0:WINDOW - ts, :] = vo_ref[b, ts:WINDOW, :]
        ko_ref[b, WINDOW - ts:WINDOW, :] = kn[r0:r0 + ts]
        vo_ref[b, WINDOW - ts:WINDOW, :] = v[r0:r0 + ts]

    att = jnp.concatenate(att_rows, axis=0).astype(BF16)
    hm = jnp.concatenate(hl_rows, axis=0)
    gated = jax.nn.sigmoid(mo) * hm
    glo = glo_ref[...]
    hl = jnp.concatenate(
        [_rms(gated[:, h * LSTM_HEAD_DIM:(h + 1) * LSTM_HEAD_DIM],
              glo[:, h * LSTM_HEAD_DIM:(h + 1) * LSTM_HEAD_DIM])
         for h in range(N_LSTM_HEADS)], axis=-1).astype(BF16)
    mixed = jnp.concatenate([att, hl], axis=-1)
    x2_ref[...] = (x1 + _dot(mixed, wo_ref[...])).reshape(bb, ts, D_MODEL)

    xa = x_ref[...].reshape(rows, D_MODEL)
    x1a = _ffn_half(xa, g1_ref, w1g_ref, w1u_ref, w1d_ref)
    x1s_ref[slot_a] = x1a
    zs_ref[slot_a] = _dot(_rms(x1a, gmix_ref[...]).astype(BF16), win_ref[...])


def _ffn_kernel(x_ref, g_ref, wg_ref, wu_ref, wd_ref, gfin_ref, y_ref):
    y_ref[...] = _rms(_ffn_half(x_ref[...], g_ref, wg_ref, wu_ref, wd_ref), gfin_ref[...])


def _resident(shape):
    return pl.BlockSpec(shape, lambda *_: (0,) * len(shape), pipeline_mode=pl.Buffered(1))


def _run_mixer(cfg, x, history, sinks, weights):
    batch, seq, _ = x.shape
    bb, ts, n_seq, n_tiles = cfg.bb, cfg.ts, cfg.n_seq, cfg.n_tiles
    assert batch % bb == 0 and seq == ts * n_seq and ts % CHUNK == 0
    assert n_tiles == (batch // bb) * n_seq
    assert ts >= WINDOW or 2 * ts == WINDOW

    def tile_a(t):
        return jnp.minimum(t, n_tiles - 1)

    def tile_b(t):
        return jnp.maximum(t - 1, 0)

    def per_batch_b(*tail):
        return pl.BlockSpec((bb,) + tail, lambda t: (tile_b(t) // n_seq,) + (0,) * len(tail))

    state_specs = [
        per_batch_b(WINDOW, KV_WIDTH), per_batch_b(WINDOW, KV_WIDTH),
        per_batch_b(N_LSTM_HEADS, LSTM_HEAD_DIM, LSTM_HEAD_DIM),
        per_batch_b(N_LSTM_HEADS, LSTM_HEAD_DIM), per_batch_b(1, LANES),
    ]
    in_specs = [
        pl.BlockSpec(memory_space=pltpu.SMEM),
        pl.BlockSpec((bb, ts, D_MODEL), lambda t: (tile_a(t) // n_seq, tile_a(t) % n_seq, 0)),
    ] + (state_specs if cfg.has_history else []) + [_resident(w.shape) for w in weights]
    out_shape = (
        jax.ShapeDtypeStruct((batch, seq, D_MODEL), F32),
        jax.ShapeDtypeStruct((batch, WINDOW, KV_WIDTH), F32),
        jax.ShapeDtypeStruct((batch, WINDOW, KV_WIDTH), F32),
        jax.ShapeDtypeStruct((batch, N_LSTM_HEADS, LSTM_HEAD_DIM, LSTM_HEAD_DIM), F32),
        jax.ShapeDtypeStruct((batch, N_LSTM_HEADS, LSTM_HEAD_DIM), F32),
        jax.ShapeDtypeStruct((batch, 1, LANES), F32),
    )
    out_specs = [
        pl.BlockSpec((bb, ts, D_MODEL), lambda t: (tile_b(t) // n_seq, tile_b(t) % n_seq, 0)),
    ] + state_specs
    rows = bb * ts
    hist = pltpu.VMEM((4, bb, WINDOW + ts, LANES), BF16)
    return pl.pallas_call(
        functools.partial(_mixer_kernel, cfg),
        grid=(n_tiles + 1,),
        in_specs=in_specs, out_specs=out_specs, out_shape=out_shape,
        scratch_shapes=[hist, hist,
                        pltpu.VMEM((2, rows, D_MODEL), F32),
                        pltpu.VMEM((2, rows, _W_IN_PADDED), F32)],
        compiler_params=pltpu.CompilerParams(
            dimension_semantics=("arbitrary",), vmem_limit_bytes=_VMEM_LIMIT),
        name=f"mixer_{'sample' if cfg.has_history else 'prompt'}",
    )(sinks, x, *history, *weights)


def _run_ffn(x, weights, name):
    n_rows = x.shape[0]
    assert n_rows % _FFN_ROWS == 0
    row_spec = pl.BlockSpec((_FFN_ROWS, D_MODEL), lambda i: (i, 0))
    return pl.pallas_call(
        _ffn_kernel,
        grid=(n_rows // _FFN_ROWS,),
        in_specs=[row_spec] + [_resident(w.shape) for w in weights],
        out_specs=row_spec,
        out_shape=jax.ShapeDtypeStruct((n_rows, D_MODEL), F32),
        compiler_params=pltpu.CompilerParams(
            dimension_semantics=("arbitrary",), vmem_limit_bytes=_VMEM_LIMIT),
        name=name,
    )(x, *weights)


def _pad_gate_lanes(t):
    out = jnp.zeros((t.shape[0], 1, LANES), F32)
    return out.at[:, 0, GATE_LANE0:GATE_LANE0 + N_LSTM_HEADS].set(t)


def kernel(x_prompt, x_sample, cache_k, cache_v, state_C, state_n, state_m, g_ffn1, w1_gate, w1_up, w1_down, g_mix, w_in, b_igate, b_fgate, g_q, g_k, attn_sinks, g_lstm_out, w_out, g_ffn2, w2_gate, w2_up, w2_down, g_final):
    depth = w_in.shape[0]
    assert depth == 1
    l = 0
    batch, seq, _ = x_prompt.shape
    dec_batch, dec_seq, _ = x_sample.shape
    row = lambda t: t.reshape(1, -1).astype(F32)
    gate_pad = jnp.zeros((D_MODEL, LANES - 2 * N_LSTM_HEADS), F32)
    win = jnp.concatenate([w_in[l], gate_pad], axis=1).astype(BF16)
    gbias = jnp.concatenate(
        [b_igate[l], b_fgate[l], jnp.zeros((LANES - 2 * N_LSTM_HEADS,), F32)]).reshape(1, LANES)
    mixer_weights = (
        row(g_ffn1[l]), w1_gate[l].astype(BF16), w1_up[l].astype(BF16), w1_down[l].astype(BF16),
        row(g_mix[l]), win, gbias,
        row(jnp.tile(g_q[l], N_ATT_HEADS)), row(jnp.tile(g_k[l], N_KV_HEADS)),
        row(g_lstm_out[l]), w_out[l].astype(BF16),
    )
    ffn_weights = (
        row(g_ffn2[l]), w2_gate[l].astype(BF16), w2_up[l].astype(BF16), w2_down[l].astype(BF16),
        row(g_final[l]),
    )
    sinks = attn_sinks[l].astype(F32)

    bb = 4
    prompt_cfg = _Tile(bb=bb, ts=CHUNK, n_seq=seq // CHUNK, n_tiles=(batch // bb) * (seq // CHUNK),
                       has_history=False)
    x2p, kp, vp, cp, n_p, mp = _run_mixer(prompt_cfg, x_prompt, (), sinks, mixer_weights)
    yp = _run_ffn(x2p.reshape(batch * seq, D_MODEL), ffn_weights, "ffn_prompt")

    rows = cache_k.shape[2]
    assert rows == WINDOW
    sample_cfg = _Tile(bb=bb, ts=dec_seq, n_seq=1, n_tiles=dec_batch // bb, has_history=True)
    history = (cache_k[l].reshape(dec_batch, rows, KV_WIDTH),
               cache_v[l].reshape(dec_batch, rows, KV_WIDTH),
               state_C[l], state_n[l], _pad_gate_lanes(state_m[l]))
    x2s, ks, vs, cs, n_s, ms = _run_mixer(sample_cfg, x_sample, history, sinks, mixer_weights)
    ys = _run_ffn(x2s.reshape(dec_batch * dec_seq, D_MODEL), ffn_weights, "ffn_sample")

    kv_shape = lambda b: (1, b, rows, N_KV_HEADS, HEAD_DIM)
    m_out = lambda t: t[:, 0, GATE_LANE0:GATE_LANE0 + N_LSTM_HEADS][None]
    return (yp.reshape(batch, seq, D_MODEL), ys.reshape(dec_batch, dec_seq, D_MODEL),
            kp.reshape(kv_shape(batch)), vp.reshape(kv_shape(batch)),
            cp[None], n_p[None], m_out(mp),
            ks.reshape(kv_shape(dec_batch)), vs.reshape(kv_shape(dec_batch)),
            cs[None], n_s[None], m_out(ms))
```

```python
import dataclasses
import functools

import jax
import jax.numpy as jnp
from jax import lax
from jax.experimental import pallas as pl
from jax.experimental.pallas import tpu as pltpu

F32 = jnp.float32
BF16 = jnp.bfloat16

D_MODEL = 1024
D_FF = 2816
CHUNK = 64
WINDOW = 128
N_KEYS = WINDOW + CHUNK
N_ATT_HEADS = 8
HEAD_DIM = 64
N_KV_HEADS = 2
GQA_GROUP = N_ATT_HEADS // N_KV_HEADS
ATT_WIDTH = N_ATT_HEADS * HEAD_DIM
KV_WIDTH = N_KV_HEADS * HEAD_DIM
N_LSTM_HEADS = 4
LSTM_HEAD_DIM = 128
LSTM_WIDTH = N_LSTM_HEADS * LSTM_HEAD_DIM
EPS = 1e-6
LANES = 128
GATE_LANE0 = 4

_OFF_Q = 0
_OFF_K = _OFF_Q + ATT_WIDTH
_OFF_V = _OFF_K + KV_WIDTH
_OFF_MQ = _OFF_V + KV_WIDTH
_OFF_MK = _OFF_MQ + LSTM_WIDTH
_OFF_MV = _OFF_MK + LSTM_WIDTH
_OFF_MO = _OFF_MV + LSTM_WIDTH
_OFF_GATE = _OFF_MO + LSTM_WIDTH
_W_IN_PADDED = _OFF_GATE + LANES

_V7X_VMEM_BYTES = 64 * 1024 * 1024
_VMEM_LIMIT = _V7X_VMEM_BYTES - 2 * 1024 * 1024
_FFN_ROWS = 1024
_FFN_SUB_ROWS = 256
_FFN_BLOCK = 512
_FFN_SPLITS = tuple((c, min(c + _FFN_BLOCK, D_FF)) for c in range(0, D_FF, _FFN_BLOCK))
_A_BLOCKS_PER_PHASE = (2, 2, 1, 1)
_IN_PROJ_SPLITS = ((0, 1024), (1024, 2048), (2048, _W_IN_PADDED))
_NT = (((1,), (1,)), ((), ()))
_TN = (((0,), (0,)), ((), ()))


@dataclasses.dataclass(frozen=True)
class _Tile:
    bb: int
    ts: int
    n_seq: int
    n_tiles: int
    has_history: bool


def _dot(a, b):
    return jnp.dot(a, b, preferred_element_type=F32)


def _rms(x, g):
    ms = jnp.mean(x * x, axis=-1, keepdims=True)
    return x * lax.rsqrt(ms + EPS) * g


def _rms_head_pairs(x, g, lo):
    outs = []
    for j in range(x.shape[-1] // LANES):
        t = x[:, j * LANES:(j + 1) * LANES]
        y = t * t
        s_lo = jnp.sum(jnp.where(lo, y, 0.0), axis=-1, keepdims=True)
        s_hi = jnp.sum(jnp.where(lo, 0.0, y), axis=-1, keepdims=True)
        inv = jnp.where(lo, lax.rsqrt(s_lo * (1.0 / HEAD_DIM) + EPS),
                        lax.rsqrt(s_hi * (1.0 / HEAD_DIM) + EPS))
        outs.append(t * inv * g[:, j * LANES:(j + 1) * LANES])
    return jnp.concatenate(outs, axis=-1)


def _lane_column(x, sel):
    return jnp.sum(jnp.where(sel, x, 0.0), axis=1, keepdims=True)


def _ffn_half(x, g_ref, wg_ref, wu_ref, wd_ref):
    h = _rms(x, g_ref[...]).astype(BF16)
    gate = _dot(h, wg_ref[...])
    up = _dot(h, wu_ref[...])
    act = (gate * jax.nn.sigmoid(gate) * up).astype(BF16)
    return x + 0.5 * _dot(act, wd_ref[...])


def _kv_variants(t, lo):
    r = pltpu.roll(t, HEAD_DIM, axis=1)
    z = jnp.zeros_like(t)
    return (jnp.where(lo, t, z).astype(BF16), jnp.where(lo, z, r).astype(BF16),
            jnp.where(lo, r, z).astype(BF16), jnp.where(lo, z, t).astype(BF16))


def _cumsum_chunks(x, tri):
    hi = x.astype(BF16)
    r1 = x - hi.astype(F32)
    mid = r1.astype(BF16)
    lo = (r1 - mid.astype(F32)).astype(BF16)
    return _dot(tri, hi) + _dot(tri, mid) + _dot(tri, lo)


def _mixer_kernel(cfg, *refs):
    it = iter(refs)
    sink_ref, x_ref = next(it), next(it)
    if cfg.has_history:
        ck_ref, cv_ref, c0_ref, n0_ref, m0_ref = (next(it) for _ in range(5))
    (g1_ref, w1g_ref, w1u_ref, w1d_ref, gmix_ref, win_ref, gbias_ref,
     gq_ref, gk_ref, glo_ref, wo_ref) = (next(it) for _ in range(11))
    x2_ref, ko_ref, vo_ref, c_ref, n_ref, m_ref = (next(it) for _ in range(6))
    kh_ref, vh_ref, x1s_ref, zs_ref = (next(it) for _ in range(4))

    bb, ts = cfg.bb, cfg.ts
    rows = bb * ts
    n_chunks = ts // CHUNK
    step = pl.program_id(0)
    slot_a = lax.rem(step, 2)
    slot_b = 1 - slot_a
    s_idx = lax.rem(jnp.maximum(step - 1, 0), cfg.n_seq)
    neg_inf = F32(-jnp.inf)

    lane = lax.broadcasted_iota(jnp.int32, (1, LANES), 1)
    lo = lane < HEAD_DIM

    @pl.when(step == 0)
    def _():
        x1s_ref[1] = jnp.zeros((rows, D_MODEL), F32)
        zs_ref[1] = jnp.zeros((rows, _W_IN_PADDED), F32)

    @pl.when(s_idx == 0)
    def _():
        for b in range(bb):
            if cfg.has_history:
                kvar = _kv_variants(ck_ref[b], lo)
                vvar = _kv_variants(cv_ref[b], lo)
                for i in range(4):
                    kh_ref[i, b, 0:WINDOW, :] = kvar[i]
                    vh_ref[i, b, 0:WINDOW, :] = vvar[i]
                ko_ref[b] = ck_ref[b]
                vo_ref[b] = cv_ref[b]
                c_ref[b] = c0_ref[b]
                n_ref[b] = n0_ref[b]
                m_ref[b] = m0_ref[b]
            else:
                zero = jnp.zeros((WINDOW, LANES), BF16)
                for i in range(4):
                    kh_ref[i, b, 0:WINDOW, :] = zero
                    vh_ref[i, b, 0:WINDOW, :] = zero
                ko_ref[b] = jnp.zeros((WINDOW, KV_WIDTH), F32)
                vo_ref[b] = jnp.zeros((WINDOW, KV_WIDTH), F32)
                c_ref[b] = jnp.zeros((N_LSTM_HEADS, LSTM_HEAD_DIM, LSTM_HEAD_DIM), F32)
                n_ref[b] = jnp.zeros((N_LSTM_HEADS, LSTM_HEAD_DIM), F32)
                m_ref[b] = jnp.zeros((1, LANES), F32)

    xa = x_ref[...].reshape(rows, D_MODEL)
    ha = _rms(xa, g1_ref[...]).astype(BF16)
    ffn_acc = []

    def stage_a_ffn(lo_col, hi_col):
        gate = _dot(ha, w1g_ref[:, lo_col:hi_col])
        up = _dot(ha, w1u_ref[:, lo_col:hi_col])
        act = (gate * jax.nn.sigmoid(gate) * up).astype(BF16)
        ffn_acc.append(_dot(act, w1d_ref[lo_col:hi_col, :]))

    def stage_a_in_proj():
        y = ffn_acc[0]
        for part in ffn_acc[1:]:
            y = y + part
        x1a = xa + 0.5 * y
        x1s_ref[slot_a] = x1a
        h2a = _rms(x1a, gmix_ref[...]).astype(BF16)
        for lo_col, hi_col in _IN_PROJ_SPLITS:
            zs_ref[slot_a, :, lo_col:hi_col] = _dot(h2a, win_ref[:, lo_col:hi_col])

    a_blocks = iter(_FFN_SPLITS)

    def emit_a(n):
        for _ in range(n):
            stage_a_ffn(*next(a_blocks))

    def proj(off, width):
        return zs_ref[slot_b, :, off:off + width]

    x1 = x1s_ref[slot_b]
    q = _rms_head_pairs(proj(_OFF_Q, ATT_WIDTH), gq_ref[...], lo)
    qb = (q * (HEAD_DIM ** -0.5)).astype(BF16)
    kn = _rms_head_pairs(proj(_OFF_K, KV_WIDTH), gk_ref[...], lo)
    v = proj(_OFF_V, KV_WIDTH)
    mq = proj(_OFF_MQ, LSTM_WIDTH)
    mqb = mq.astype(BF16)
    mk = proj(_OFF_MK, LSTM_WIDTH) * (LSTM_HEAD_DIM ** -0.5)
    mkb = mk.astype(BF16)
    mvb = proj(_OFF_MV, LSTM_WIDTH).astype(BF16)
    mo = proj(_OFF_MO, LSTM_WIDTH)
    gates = proj(_OFF_GATE, LANES) + gbias_ref[...]

    log_f = jnp.minimum(gates, 0.0) - jnp.log1p(jnp.exp(-jnp.abs(gates)))
    log_i = pltpu.roll(gates, GATE_LANE0, axis=1)
    ri = lax.broadcasted_iota(jnp.int32, (rows, rows), 0)
    ci = lax.broadcasted_iota(jnp.int32, (rows, rows), 1)
    tri = ((ci <= ri) & ((ci // CHUNK) == (ri // CHUNK))).astype(BF16)
    b_all = _cumsum_chunks(log_f, tri)
    emit_a(_A_BLOCKS_PER_PHASE[0])
    r_all = log_i - b_all
    r_rows = r_all.T

    qi = lax.broadcasted_iota(jnp.int32, (CHUNK, N_KEYS), 0)
    kj = lax.broadcasted_iota(jnp.int32, (CHUNK, N_KEYS), 1)
    dist = jnp.abs(WINDOW + qi - kj).astype(F32)
    half = lax.broadcasted_iota(jnp.int32, (2 * CHUNK, 1), 0) < CHUNK
    alibi, sink_col = [], []
    for kh in range(N_KV_HEADS):
        for pos in range(2):
            h_a = GQA_GROUP * kh + pos
            h_b = h_a + 2
            alibi.append(jnp.concatenate(
                [dist * (2.0 ** -(h_a + 1)), dist * (2.0 ** -(h_b + 1))], axis=0))
            sink_col.append(jnp.where(half, sink_ref[h_a], sink_ref[h_b]))
    key_idx = lax.broadcasted_iota(jnp.int32, (1, N_KEYS), 1)
    ti = lax.broadcasted_iota(jnp.int32, (CHUNK, CHUNK), 0)
    si = lax.broadcasted_iota(jnp.int32, (CHUNK, CHUNK), 1)
    causal = si <= ti

    assert n_chunks == 1, "stage B is written phase-major for one chunk per batch row"
    row0 = [b * ts for b in range(bb)]
    att_jobs = [(b, kh, pos) for b in range(bb) for kh in range(N_KV_HEADS) for pos in range(2)]
    jobs = [(b, h) for b in range(bb) for h in range(N_LSTM_HEADS)]
    hsl = lambda h: slice(h * LSTM_HEAD_DIM, (h + 1) * LSTM_HEAD_DIM)
    rsl = lambda b: slice(row0[b], row0[b] + CHUNK)

    for b in range(bb):
        kvar = _kv_variants(kn[row0[b]:row0[b] + ts], lo)
        vvar = _kv_variants(v[row0[b]:row0[b] + ts], lo)
        for i in range(4):
            kh_ref[i, b, WINDOW:WINDOW + ts, :] = kvar[i]
            vh_ref[i, b, WINDOW:WINDOW + ts, :] = vvar[i]
    scores = []
    for b, kh, pos in att_jobs:
        rs = row0[b]
        qs = jnp.concatenate(
            [qb[rs:rs + CHUNK, (2 * kh) * LANES:(2 * kh + 1) * LANES],
             qb[rs:rs + CHUNK, (2 * kh + 1) * LANES:(2 * kh + 2) * LANES]], axis=0)
        keys = kh_ref[2 * kh + pos, b, 0:N_KEYS, :]
        scores.append(lax.dot_general(qs, keys, _NT, preferred_element_type=F32))
    c_old = {j: c_ref[j[0], j[1]] for j in jobs}
    n_old = {j: n_ref[j[0], j[1]:j[1] + 1, :] for j in jobs}
    qk = {(b, h): lax.dot_general(mqb[rsl(b), hsl(h)], mkb[rsl(b), hsl(h)], _NT,
                                  preferred_element_type=F32) for b, h in jobs}
    q_c = {(b, h): _dot(mqb[rsl(b), hsl(h)], c_old[(b, h)].astype(BF16)) for b, h in jobs}
    emit_a(_A_BLOCKS_PER_PHASE[1])

    if cfg.has_history:
        bias = None
    else:
        first_valid = (WINDOW // CHUNK - s_idx) * CHUNK
        bias = jnp.where(key_idx >= first_valid, 0.0, neg_inf)
    probs, dens = [], []
    for (b, kh, pos), sc in zip(att_jobs, scores):
        i = 2 * kh + pos
        sc = sc - alibi[i]
        if bias is not None:
            sc = sc + bias
        mx = jnp.maximum(jnp.max(sc, axis=-1, keepdims=True), sink_col[i])
        p = jnp.exp(sc - mx)
        dens.append(jnp.sum(p, axis=-1, keepdims=True) + jnp.exp(sink_col[i] - mx))
        probs.append(p.astype(BF16))
    m_rows = [m_ref[b] for b in range(bb)]
    bcs = [b_all[rsl(b)] for b in range(bb)]
    rcs = [r_all[rsl(b)] for b in range(bb)]
    inters, decays, wss, m_news = [], [], [], []
    for b in range(bb):
        tot = bcs[b][CHUNK - 1:CHUNK, :]
        g = tot + rcs[b]
        m_new = jnp.maximum(tot + m_rows[b], jnp.max(g, axis=0, keepdims=True))
        inters.append(bcs[b] + m_rows[b])
        decays.append(jnp.exp(tot + m_rows[b] - m_new))
        wss.append(jnp.exp(g - m_new))
        m_news.append(m_new)
    a_mat, mts, w_inters = {}, {}, {}
    for b, h in jobs:
        sel = lane == GATE_LANE0 + h
        b_col = _lane_column(bcs[b], sel)
        inter_col = _lane_column(inters[b], sel)
        r_row = r_rows[GATE_LANE0 + h:GATE_LANE0 + h + 1, rsl(b)]
        d = jnp.where(causal, b_col + r_row, neg_inf)
        mt = jnp.maximum(inter_col, jnp.max(d, axis=1, keepdims=True))
        a_mat[(b, h)] = jnp.exp(d - mt) * qk[(b, h)]
        mts[(b, h)] = mt
        w_inters[(b, h)] = jnp.exp(inter_col - mt)
    wks = {}
    for b, h in jobs:
        wks[(b, h)] = _lane_column(wss[b], lane == GATE_LANE0 + h) * mk[rsl(b), hsl(h)]

    att_o = [_dot(p, vh_ref[2 * kh + pos, b, 0:N_KEYS, :])
             for (b, kh, pos), p in zip(att_jobs, probs)]
    intra = {(b, h): _dot(a_mat[(b, h)].astype(BF16), mvb[rsl(b), hsl(h)]) for b, h in jobs}
    c_upd = {(b, h): lax.dot_general(wks[(b, h)].astype(BF16), mvb[rsl(b), hsl(h)], _TN,
                                     preferred_element_type=F32) for b, h in jobs}
    emit_a(_A_BLOCKS_PER_PHASE[2])

    pair_out = [[None] * 4 for _ in range(bb)]
    for (b, kh, pos), o, den in zip(att_jobs, att_o, dens):
        o = o / den
        for part in range(2):
            j = 2 * kh + part
            piece = o[part * CHUNK:(part + 1) * CHUNK]
            pair_out[b][j] = piece if pair_out[b][j] is None else pair_out[b][j] + piece
    att_rows = [jnp.concatenate(pair_out[b], axis=-1) for b in range(bb)]
    head_out = {}
    for b, h in jobs:
        j = (b, h)
        q_n = jnp.sum(mq[rsl(b), hsl(h)] * n_old[j], axis=1, keepdims=True)
        num = intra[j] + w_inters[j] * q_c[j]
        den = jnp.sum(a_mat[j], axis=1, keepdims=True) + w_inters[j] * q_n
        head_out[j] = num / jnp.maximum(jnp.abs(den), jnp.exp(-mts[j]))
    for b, h in jobs:
        j = (b, h)
        dec = _lane_column(decays[b], lane == GATE_LANE0 + h)
        c_ref[b, h] = dec * c_old[j] + c_upd[j]
        n_ref[b, h:h + 1, :] = dec * n_old[j] + jnp.sum(wks[j], axis=0, keepdims=True)
    hl_rows = [jnp.concatenate([head_out[(b, h)] for h in range(N_LSTM_HEADS)], axis=-1)
               for b in range(bb)]
    for b in range(bb):
        r0 = row0[b]
        m_ref[b] = m_news[b]
        if cfg.n_seq > 1:
            for i in range(4):
                kh_ref[i, b, 0:WINDOW, :] = kh_ref[i, b, ts:ts + WINDOW, :]
                vh_ref[i, b, 0:WINDOW, :] = vh_ref[i, b, ts:ts + WINDOW, :]
        ko_ref[b, 0:WINDOW - ts, :] = ko_ref[b, ts:WINDOW, :]
        vo_ref[b, 0:WINDOW - ts, :] = vo_ref[b, ts:WINDOW, :]
        ko_ref[b, WINDOW - ts:WINDOW, :] = kn[r0:r0 + ts]
        vo_ref[b, WINDOW - ts:WINDOW, :] = v[r0:r0 + ts]
    emit_a(_A_BLOCKS_PER_PHASE[3])

    att = jnp.concatenate(att_rows, axis=0).astype(BF16)
    hm = jnp.concatenate(hl_rows, axis=0)
    gated = jax.nn.sigmoid(mo) * hm
    glo = glo_ref[...]
    hl = jnp.concatenate(
        [_rms(gated[:, hsl(h)], glo[:, hsl(h)]) for h in range(N_LSTM_HEADS)],
        axis=-1).astype(BF16)
    mixed = jnp.concatenate([att, hl], axis=-1)
    x2_ref[...] = (x1 + _dot(mixed, wo_ref[...])).reshape(bb, ts, D_MODEL)

    stage_a_in_proj()
    assert next(a_blocks, None) is None


def _ffn_kernel(x_ref, g_ref, wg_ref, wu_ref, wd_ref, gfin_ref, y_ref):
    for i in range(_FFN_ROWS // _FFN_SUB_ROWS):
        sl = slice(i * _FFN_SUB_ROWS, (i + 1) * _FFN_SUB_ROWS)
        y_ref[sl, :] = _rms(_ffn_half(x_ref[sl, :], g_ref, wg_ref, wu_ref, wd_ref), gfin_ref[...])


def _resident(shape):
    return pl.BlockSpec(shape, lambda *_: (0,) * len(shape), pipeline_mode=pl.Buffered(1))


def _run_mixer(cfg, x, history, sinks, weights):
    batch, seq, _ = x.shape
    bb, ts, n_seq, n_tiles = cfg.bb, cfg.ts, cfg.n_seq, cfg.n_tiles
    assert batch % bb == 0 and seq == ts * n_seq and ts % CHUNK == 0
    assert n_tiles == (batch // bb) * n_seq
    assert ts >= WINDOW or 2 * ts == WINDOW

    def tile_a(t):
        return jnp.minimum(t, n_tiles - 1)

    def tile_b(t):
        return jnp.maximum(t - 1, 0)

    def per_batch_b(*tail):
        return pl.BlockSpec((bb,) + tail, lambda t: (tile_b(t) // n_seq,) + (0,) * len(tail))

    state_specs = [
        per_batch_b(WINDOW, KV_WIDTH), per_batch_b(WINDOW, KV_WIDTH),
        per_batch_b(N_LSTM_HEADS, LSTM_HEAD_DIM, LSTM_HEAD_DIM),
        per_batch_b(N_LSTM_HEADS, LSTM_HEAD_DIM), per_batch_b(1, LANES),
    ]
    in_specs = [
        pl.BlockSpec(memory_space=pltpu.SMEM),
        pl.BlockSpec((bb, ts, D_MODEL), lambda t: (tile_a(t) // n_seq, tile_a(t) % n_seq, 0)),
    ] + (state_specs if cfg.has_history else []) + [_resident(w.shape) for w in weights]
    out_shape = (
        jax.ShapeDtypeStruct((batch, seq, D_MODEL), F32),
        jax.ShapeDtypeStruct((batch, WINDOW, KV_WIDTH), F32),
        jax.ShapeDtypeStruct((batch, WINDOW, KV_WIDTH), F32),
        jax.ShapeDtypeStruct((batch, N_LSTM_HEADS, LSTM_HEAD_DIM, LSTM_HEAD_DIM), F32),
        jax.ShapeDtypeStruct((batch, N_LSTM_HEADS, LSTM_HEAD_DIM), F32),
        jax.ShapeDtypeStruct((batch, 1, LANES), F32),
    )
    out_specs = [
        pl.BlockSpec((bb, ts, D_MODEL), lambda t: (tile_b(t) // n_seq, tile_b(t) % n_seq, 0)),
    ] + state_specs
    rows = bb * ts
    hist = pltpu.VMEM((4, bb, WINDOW + ts, LANES), BF16)
    return pl.pallas_call(
        functools.partial(_mixer_kernel, cfg),
        grid=(n_tiles + 1,),
        in_specs=in_specs, out_specs=out_specs, out_shape=out_shape,
        scratch_shapes=[hist, hist,
                        pltpu.VMEM((2, rows, D_MODEL), F32),
                        pltpu.VMEM((2, rows, _W_IN_PADDED), F32)],
        compiler_params=pltpu.CompilerParams(
            dimension_semantics=("arbitrary",), vmem_limit_bytes=_VMEM_LIMIT),
        name=f"mixer_{'sample' if cfg.has_history else 'prompt'}",
    )(sinks, x, *history, *weights)


def _run_ffn(x, weights, name):
    n_rows = x.shape[0]
    assert n_rows % _FFN_ROWS == 0
    row_spec = pl.BlockSpec((_FFN_ROWS, D_MODEL), lambda i: (i, 0))
    return pl.pallas_call(
        _ffn_kernel,
        grid=(n_rows // _FFN_ROWS,),
        in_specs=[row_spec] + [_resident(w.shape) for w in weights],
        out_specs=row_spec,
        out_shape=jax.ShapeDtypeStruct((n_rows, D_MODEL), F32),
        compiler_params=pltpu.CompilerParams(
            dimension_semantics=("arbitrary",), vmem_limit_bytes=_VMEM_LIMIT),
        name=name,
    )(x, *weights)


def _pad_gate_lanes(t):
    out = jnp.zeros((t.shape[0], 1, LANES), F32)
    return out.at[:, 0, GATE_LANE0:GATE_LANE0 + N_LSTM_HEADS].set(t)


def kernel(x_prompt, x_sample, cache_k, cache_v, state_C, state_n, state_m, g_ffn1, w1_gate, w1_up, w1_down, g_mix, w_in, b_igate, b_fgate, g_q, g_k, attn_sinks, g_lstm_out, w_out, g_ffn2, w2_gate, w2_up, w2_down, g_final):
    depth = w_in.shape[0]
    assert depth == 1
    l = 0
    batch, seq, _ = x_prompt.shape
    dec_batch, dec_seq, _ = x_sample.shape
    row = lambda t: t.reshape(1, -1).astype(F32)
    gate_pad = jnp.zeros((D_MODEL, LANES - 2 * N_LSTM_HEADS), F32)
    win = jnp.concatenate([w_in[l], gate_pad], axis=1).astype(BF16)
    gbias = jnp.concatenate(
        [b_igate[l], b_fgate[l], jnp.zeros((LANES - 2 * N_LSTM_HEADS,), F32)]).reshape(1, LANES)
    mixer_weights = (
        row(g_ffn1[l]), w1_gate[l].astype(BF16), w1_up[l].astype(BF16), w1_down[l].astype(BF16),
        row(g_mix[l]), win, gbias,
        row(jnp.tile(g_q[l], N_ATT_HEADS)), row(jnp.tile(g_k[l], N_KV_HEADS)),
        row(g_lstm_out[l]), w_out[l].astype(BF16),
    )
    ffn_weights = (
        row(g_ffn2[l]), w2_gate[l].astype(BF16), w2_up[l].astype(BF16), w2_down[l].astype(BF16),
        row(g_final[l]),
    )
    sinks = attn_sinks[l].astype(F32)

    bb = 4
    prompt_cfg = _Tile(bb=bb, ts=CHUNK, n_seq=seq // CHUNK, n_tiles=(batch // bb) * (seq // CHUNK),
                       has_history=False)
    x2p, kp, vp, cp, n_p, mp = _run_mixer(prompt_cfg, x_prompt, (), sinks, mixer_weights)
    yp = _run_ffn(x2p.reshape(batch * seq, D_MODEL), ffn_weights, "ffn_prompt")

    rows = cache_k.shape[2]
    assert rows == WINDOW
    sample_cfg = _Tile(bb=bb, ts=dec_seq, n_seq=1, n_tiles=dec_batch // bb, has_history=True)
    history = (cache_k[l].reshape(dec_batch, rows, KV_WIDTH),
               cache_v[l].reshape(dec_batch, rows, KV_WIDTH),
               state_C[l], state_n[l], _pad_gate_lanes(state_m[l]))
    x2s, ks, vs, cs, n_s, ms = _run_mixer(sample_cfg, x_sample, history, sinks, mixer_weights)
    ys = _run_ffn(x2s.reshape(dec_batch * dec_seq, D_MODEL), ffn_weights, "ffn_sample")

    kv_shape = lambda b: (1, b, rows, N_KV_HEADS, HEAD_DIM)
    m_out = lambda t: t[:, 0, GATE_LANE0:GATE_LANE0 + N_LSTM_HEADS][None]
    return (yp.reshape(batch, seq, D_MODEL), ys.reshape(dec_batch, dec_seq, D_MODEL),
            kp.reshape(kv_shape(batch)), vp.reshape(kv_shape(batch)),
            cp[None], n_p[None], m_out(mp),
            ks.reshape(kv_shape(dec_batch)), vs.reshape(kv_shape(dec_batch)),
            cs[None], n_s[None], m_out(ms))
```

```python
import dataclasses
import functools

import jax
import jax.numpy as jnp
from jax import lax
from jax.experimental import pallas as pl
from jax.experimental.pallas import tpu as pltpu

F32 = jnp.float32
BF16 = jnp.bfloat16

D_MODEL = 1024
D_FF = 2816
CHUNK = 64
WINDOW = 128
N_KEYS = WINDOW + CHUNK
N_ATT_HEADS = 8
HEAD_DIM = 64
N_KV_HEADS = 2
GQA_GROUP = N_ATT_HEADS // N_KV_HEADS
ATT_WIDTH = N_ATT_HEADS * HEAD_DIM
KV_WIDTH = N_KV_HEADS * HEAD_DIM
N_LSTM_HEADS = 4
LSTM_HEAD_DIM = 128
LSTM_WIDTH = N_LSTM_HEADS * LSTM_HEAD_DIM
EPS = 1e-6
LANES = 128
GATE_LANE0 = 4

_OFF_Q = 0
_OFF_K = _OFF_Q + ATT_WIDTH
_OFF_V = _OFF_K + KV_WIDTH
_OFF_MQ = _OFF_V + KV_WIDTH
_OFF_MK = _OFF_MQ + LSTM_WIDTH
_OFF_MV = _OFF_MK + LSTM_WIDTH
_OFF_MO = _OFF_MV + LSTM_WIDTH
_OFF_GATE = _OFF_MO + LSTM_WIDTH
_W_IN_PADDED = _OFF_GATE + LANES

_V7X_VMEM_BYTES = 64 * 1024 * 1024
_VMEM_LIMIT = _V7X_VMEM_BYTES - 2 * 1024 * 1024
_FFN_ROWS = 1024
_FFN_SUB_ROWS = 256
_FFN_BLOCK = 512
_FFN_SPLITS = tuple((c, min(c + _FFN_BLOCK, D_FF)) for c in range(0, D_FF, _FFN_BLOCK))
_A_BLOCKS_PER_PHASE = (2, 2, 1, 1)
_IN_PROJ_SPLITS = ((0, 1024), (1024, 2048), (2048, _W_IN_PADDED))
_NT = (((1,), (1,)), ((), ()))
_TN = (((0,), (0,)), ((), ()))


@dataclasses.dataclass(frozen=True)
class _Tile:
    bb: int
    ts: int
    n_seq: int
    n_tiles: int
    has_history: bool


def _dot(a, b):
    return jnp.dot(a, b, preferred_element_type=F32)


def _rms(x, g):
    ms = jnp.mean(x * x, axis=-1, keepdims=True)
    return x * lax.rsqrt(ms + EPS) * g


def _rms_head_pairs(x, g, lo):
    outs = []
    for j in range(x.shape[-1] // LANES):
        t = x[:, j * LANES:(j + 1) * LANES]
        y = t * t
        s_lo = jnp.sum(jnp.where(lo, y, 0.0), axis=-1, keepdims=True)
        s_hi = jnp.sum(jnp.where(lo, 0.0, y), axis=-1, keepdims=True)
        inv = jnp.where(lo, lax.rsqrt(s_lo * (1.0 / HEAD_DIM) + EPS),
                        lax.rsqrt(s_hi * (1.0 / HEAD_DIM) + EPS))
        outs.append(t * inv * g[:, j * LANES:(j + 1) * LANES])
    return jnp.concatenate(outs, axis=-1)


def _lane_column(x, sel):
    return jnp.sum(jnp.where(sel, x, 0.0), axis=1, keepdims=True)


def _ffn_half(x, g_ref, wg_ref, wu_ref, wd_ref):
    h = _rms(x, g_ref[...]).astype(BF16)
    gate = _dot(h, wg_ref[...])
    up = _dot(h, wu_ref[...])
    act = (gate * jax.nn.sigmoid(gate) * up).astype(BF16)
    return x + 0.5 * _dot(act, wd_ref[...])


def _kv_variants(t, lo):
    r = pltpu.roll(t, HEAD_DIM, axis=1)
    z = jnp.zeros_like(t)
    return (jnp.where(lo, t, z).astype(BF16), jnp.where(lo, z, r).astype(BF16),
            jnp.where(lo, r, z).astype(BF16), jnp.where(lo, z, t).astype(BF16))


def _cumsum_chunks(x, tri):
    hi = x.astype(BF16)
    r1 = x - hi.astype(F32)
    mid = r1.astype(BF16)
    lo = (r1 - mid.astype(F32)).astype(BF16)
    return _dot(tri, hi) + _dot(tri, mid) + _dot(tri, lo)


def _mixer_kernel(cfg, *refs):
    it = iter(refs)
    sink_ref, x_ref = next(it), next(it)
    if cfg.has_history:
        ck_ref, cv_ref, c0_ref, n0_ref, m0_ref = (next(it) for _ in range(5))
    (g1_ref, w1g_ref, w1u_ref, w1d_ref, gmix_ref, win_ref, gbias_ref,
     gq_ref, gk_ref, glo_ref, wo_ref) = (next(it) for _ in range(11))
    x2_ref, ko_ref, vo_ref, c_ref, n_ref, m_ref = (next(it) for _ in range(6))
    kh_ref, vh_ref, x1s_ref, zs_ref = (next(it) for _ in range(4))

    bb, ts = cfg.bb, cfg.ts
    rows = bb * ts
    n_chunks = ts // CHUNK
    step = pl.program_id(0)
    slot_a = lax.rem(step, 2)
    slot_b = 1 - slot_a
    s_idx = lax.rem(jnp.maximum(step - 1, 0), cfg.n_seq)
    neg_inf = F32(-jnp.inf)

    lane = lax.broadcasted_iota(jnp.int32, (1, LANES), 1)
    lo = lane < HEAD_DIM

    @pl.when(step == 0)
    def _():
        x1s_ref[1] = jnp.zeros((rows, D_MODEL), F32)
        zs_ref[1] = jnp.zeros((rows, _W_IN_PADDED), F32)

    @pl.when(s_idx == 0)
    def _():
        for b in range(bb):
            if cfg.has_history:
                kvar = _kv_variants(ck_ref[b], lo)
                vvar = _kv_variants(cv_ref[b], lo)
                for i in range(4):
                    kh_ref[i, b, 0:WINDOW, :] = kvar[i]
                    vh_ref[i, b, 0:WINDOW, :] = vvar[i]
                ko_ref[b] = ck_ref[b]
                vo_ref[b] = cv_ref[b]
                c_ref[b] = c0_ref[b]
                n_ref[b] = n0_ref[b]
                m_ref[b] = m0_ref[b]
            else:
                zero = jnp.zeros((WINDOW, LANES), BF16)
                for i in range(4):
                    kh_ref[i, b, 0:WINDOW, :] = zero
                    vh_ref[i, b, 0:WINDOW, :] = zero
                ko_ref[b] = jnp.zeros((WINDOW, KV_WIDTH), F32)
                vo_ref[b] = jnp.zeros((WINDOW, KV_WIDTH), F32)
                c_ref[b] = jnp.zeros((N_LSTM_HEADS, LSTM_HEAD_DIM, LSTM_HEAD_DIM), F32)
                n_ref[b] = jnp.zeros((N_LSTM_HEADS, LSTM_HEAD_DIM), F32)
                m_ref[b] = jnp.zeros((1, LANES), F32)

    xa = x_ref[...].reshape(rows, D_MODEL)
    ha = _rms(xa, g1_ref[...]).astype(BF16)
    ffn_acc = []

    pending_act = []

    def stage_a_ffn(lo_col, hi_col):
        gate = _dot(ha, w1g_ref[:, lo_col:hi_col])
        up = _dot(ha, w1u_ref[:, lo_col:hi_col])
        stage_a_down()
        pending_act.append(((gate * jax.nn.sigmoid(gate) * up).astype(BF16), lo_col, hi_col))

    def stage_a_down():
        if pending_act:
            act, lo_col, hi_col = pending_act.pop()
            ffn_acc.append(_dot(act, w1d_ref[lo_col:hi_col, :]))

    def stage_a_in_proj():
        stage_a_down()
        y = ffn_acc[0]
        for part in ffn_acc[1:]:
            y = y + part
        x1a = xa + 0.5 * y
        x1s_ref[slot_a] = x1a
        h2a = _rms(x1a, gmix_ref[...]).astype(BF16)
        for lo_col, hi_col in _IN_PROJ_SPLITS:
            zs_ref[slot_a, :, lo_col:hi_col] = _dot(h2a, win_ref[:, lo_col:hi_col])

    a_blocks = iter(_FFN_SPLITS)

    def emit_a(n):
        for _ in range(n):
            stage_a_ffn(*next(a_blocks))

    def proj(off, width):
        return zs_ref[slot_b, :, off:off + width]

    x1 = x1s_ref[slot_b]
    q = _rms_head_pairs(proj(_OFF_Q, ATT_WIDTH), gq_ref[...], lo)
    qb = (q * (HEAD_DIM ** -0.5)).astype(BF16)
    kn = _rms_head_pairs(proj(_OFF_K, KV_WIDTH), gk_ref[...], lo)
    v = proj(_OFF_V, KV_WIDTH)
    mq = proj(_OFF_MQ, LSTM_WIDTH)
    mqb = mq.astype(BF16)
    mk = proj(_OFF_MK, LSTM_WIDTH) * (LSTM_HEAD_DIM ** -0.5)
    mkb = mk.astype(BF16)
    mvb = proj(_OFF_MV, LSTM_WIDTH).astype(BF16)
    mo = proj(_OFF_MO, LSTM_WIDTH)
    gates = proj(_OFF_GATE, LANES) + gbias_ref[...]

    log_f = jnp.minimum(gates, 0.0) - jnp.log1p(jnp.exp(-jnp.abs(gates)))
    log_i = pltpu.roll(gates, GATE_LANE0, axis=1)
    ri = lax.broadcasted_iota(jnp.int32, (rows, rows), 0)
    ci = lax.broadcasted_iota(jnp.int32, (rows, rows), 1)
    tri = ((ci <= ri) & ((ci // CHUNK) == (ri // CHUNK))).astype(BF16)
    b_all = _cumsum_chunks(log_f, tri)
    emit_a(_A_BLOCKS_PER_PHASE[0])
    r_all = log_i - b_all
    r_rows = r_all.T

    qi = lax.broadcasted_iota(jnp.int32, (CHUNK, N_KEYS), 0)
    kj = lax.broadcasted_iota(jnp.int32, (CHUNK, N_KEYS), 1)
    dist = jnp.abs(WINDOW + qi - kj).astype(F32)
    half = lax.broadcasted_iota(jnp.int32, (2 * CHUNK, 1), 0) < CHUNK
    alibi, sink_col = [], []
    for kh in range(N_KV_HEADS):
        for pos in range(2):
            h_a = GQA_GROUP * kh + pos
            h_b = h_a + 2
            alibi.append(jnp.concatenate(
                [dist * (2.0 ** -(h_a + 1)), dist * (2.0 ** -(h_b + 1))], axis=0))
            sink_col.append(jnp.where(half, sink_ref[h_a], sink_ref[h_b]))
    key_idx = lax.broadcasted_iota(jnp.int32, (1, N_KEYS), 1)
    ti = lax.broadcasted_iota(jnp.int32, (CHUNK, CHUNK), 0)
    si = lax.broadcasted_iota(jnp.int32, (CHUNK, CHUNK), 1)
    causal = si <= ti

    assert n_chunks == 1, "stage B is written phase-major for one chunk per batch row"
    row0 = [b * ts for b in range(bb)]
    att_jobs = [(b, kh, pos) for b in range(bb) for kh in range(N_KV_HEADS) for pos in range(2)]
    jobs = [(b, h) for b in range(bb) for h in range(N_LSTM_HEADS)]
    hsl = lambda h: slice(h * LSTM_HEAD_DIM, (h + 1) * LSTM_HEAD_DIM)
    rsl = lambda b: slice(row0[b], row0[b] + CHUNK)

    for b in range(bb):
        kvar = _kv_variants(kn[row0[b]:row0[b] + ts], lo)
        vvar = _kv_variants(v[row0[b]:row0[b] + ts], lo)
        for i in range(4):
            kh_ref[i, b, WINDOW:WINDOW + ts, :] = kvar[i]
            vh_ref[i, b, WINDOW:WINDOW + ts, :] = vvar[i]
    scores = []
    for b, kh, pos in att_jobs:
        rs = row0[b]
        qs = jnp.concatenate(
            [qb[rs:rs + CHUNK, (2 * kh) * LANES:(2 * kh + 1) * LANES],
             qb[rs:rs + CHUNK, (2 * kh + 1) * LANES:(2 * kh + 2) * LANES]], axis=0)
        keys = kh_ref[2 * kh + pos, b, 0:N_KEYS, :]
        scores.append(lax.dot_general(qs, keys, _NT, preferred_element_type=F32))
    c_old = {j: c_ref[j[0], j[1]] for j in jobs}
    n_old = {j: n_ref[j[0], j[1]:j[1] + 1, :] for j in jobs}
    qk = {(b, h): lax.dot_general(mqb[rsl(b), hsl(h)], mkb[rsl(b), hsl(h)], _NT,
                                  preferred_element_type=F32) for b, h in jobs}
    q_c = {(b, h): _dot(mqb[rsl(b), hsl(h)], c_old[(b, h)].astype(BF16)) for b, h in jobs}
    emit_a(_A_BLOCKS_PER_PHASE[1])

    if cfg.has_history:
        bias = None
    else:
        first_valid = (WINDOW // CHUNK - s_idx) * CHUNK
        bias = jnp.where(key_idx >= first_valid, 0.0, neg_inf)
    probs, dens = [], []
    for (b, kh, pos), sc in zip(att_jobs, scores):
        i = 2 * kh + pos
        sc = sc - alibi[i]
        if bias is not None:
            sc = sc + bias
        mx = jnp.maximum(jnp.max(sc, axis=-1, keepdims=True), sink_col[i])
        p = jnp.exp(sc - mx)
        dens.append(jnp.sum(p, axis=-1, keepdims=True) + jnp.exp(sink_col[i] - mx))
        probs.append(p.astype(BF16))
    m_rows = [m_ref[b] for b in range(bb)]
    bcs = [b_all[rsl(b)] for b in range(bb)]
    rcs = [r_all[rsl(b)] for b in range(bb)]
    inters, decays, wss, m_news = [], [], [], []
    for b in range(bb):
        tot = bcs[b][CHUNK - 1:CHUNK, :]
        g = tot + rcs[b]
        m_new = jnp.maximum(tot + m_rows[b], jnp.max(g, axis=0, keepdims=True))
        inters.append(bcs[b] + m_rows[b])
        decays.append(jnp.exp(tot + m_rows[b] - m_new))
        wss.append(jnp.exp(g - m_new))
        m_news.append(m_new)
    a_mat, mts, w_inters = {}, {}, {}
    for b, h in jobs:
        sel = lane == GATE_LANE0 + h
        b_col = _lane_column(bcs[b], sel)
        inter_col = _lane_column(inters[b], sel)
        r_row = r_rows[GATE_LANE0 + h:GATE_LANE0 + h + 1, rsl(b)]
        d = jnp.where(causal, b_col + r_row, neg_inf)
        mt = jnp.maximum(inter_col, jnp.max(d, axis=1, keepdims=True))
        a_mat[(b, h)] = jnp.exp(d - mt) * qk[(b, h)]
        mts[(b, h)] = mt
        w_inters[(b, h)] = jnp.exp(inter_col - mt)
    wks = {}
    for b, h in jobs:
        wks[(b, h)] = _lane_column(wss[b], lane == GATE_LANE0 + h) * mk[rsl(b), hsl(h)]

    att_o = [_dot(p, vh_ref[2 * kh + pos, b, 0:N_KEYS, :])
             for (b, kh, pos), p in zip(att_jobs, probs)]
    intra = {(b, h): _dot(a_mat[(b, h)].astype(BF16), mvb[rsl(b), hsl(h)]) for b, h in jobs}
    c_upd = {(b, h): lax.dot_general(wks[(b, h)].astype(BF16), mvb[rsl(b), hsl(h)], _TN,
                                     preferred_element_type=F32) for b, h in jobs}
    emit_a(_A_BLOCKS_PER_PHASE[2])

    pair_out = [[None] * 4 for _ in range(bb)]
    for (b, kh, pos), o, den in zip(att_jobs, att_o, dens):
        o = o / den
        for part in range(2):
            j = 2 * kh + part
            piece = o[part * CHUNK:(part + 1) * CHUNK]
            pair_out[b][j] = piece if pair_out[b][j] is None else pair_out[b][j] + piece
    att_rows = [jnp.concatenate(pair_out[b], axis=-1) for b in range(bb)]
    head_out = {}
    for b, h in jobs:
        j = (b, h)
        q_n = jnp.sum(mq[rsl(b), hsl(h)] * n_old[j], axis=1, keepdims=True)
        num = intra[j] + w_inters[j] * q_c[j]
        den = jnp.sum(a_mat[j], axis=1, keepdims=True) + w_inters[j] * q_n
        head_out[j] = num / jnp.maximum(jnp.abs(den), jnp.exp(-mts[j]))
    for b, h in jobs:
        j = (b, h)
        dec = _lane_column(decays[b], lane == GATE_LANE0 + h)
        c_ref[b, h] = dec * c_old[j] + c_upd[j]
        n_ref[b, h:h + 1, :] = dec * n_old[j] + jnp.sum(wks[j], axis=0, keepdims=True)
    hl_rows = [jnp.concatenate([head_out[(b, h)] for h in range(N_LSTM_HEADS)], axis=-1)
               for b in range(bb)]
    for b in range(bb):
        r0 = row0[b]
        m_ref[b] = m_news[b]
        if cfg.n_seq > 1:
            for i in range(4):
                kh_ref[i, b, 0:WINDOW, :] = kh_ref[i, b, ts:ts + WINDOW, :]
                vh_ref[i, b, 0:WINDOW, :] = vh_ref[i, b, ts:ts + WINDOW, :]
        ko_ref[b, 0:WINDOW - ts, :] = ko_ref[b, ts:WINDOW, :]
        vo_ref[b, 0:WINDOW - ts, :] = vo_ref[b, ts:WINDOW, :]
        ko_ref[b, WINDOW - ts:WINDOW, :] = kn[r0:r0 + ts]
        vo_ref[b, WINDOW - ts:WINDOW, :] = v[r0:r0 + ts]
    emit_a(_A_BLOCKS_PER_PHASE[3])

    att = jnp.concatenate(att_rows, axis=0).astype(BF16)
    hm = jnp.concatenate(hl_rows, axis=0)
    gated = jax.nn.sigmoid(mo) * hm
    glo = glo_ref[...]
    hl = jnp.concatenate(
        [_rms(gated[:, hsl(h)], glo[:, hsl(h)]) for h in range(N_LSTM_HEADS)],
        axis=-1).astype(BF16)
    mixed = jnp.concatenate([att, hl], axis=-1)
    x2_ref[...] = (x1 + _dot(mixed, wo_ref[...])).reshape(bb, ts, D_MODEL)

    stage_a_in_proj()
    assert next(a_blocks, None) is None


def _ffn_kernel(x_ref, g_ref, wg_ref, wu_ref, wd_ref, gfin_ref, y_ref):
    n_sub = _FFN_ROWS // _FFN_SUB_ROWS
    rows_of = lambda i: slice(i * _FFN_SUB_ROWS, (i + 1) * _FFN_SUB_ROWS)

    def gate_up(i):
        h = _rms(x_ref[rows_of(i), :], g_ref[...]).astype(BF16)
        gate = _dot(h, wg_ref[...])
        up = _dot(h, wu_ref[...])
        return gate, up

    def down(i, gate, up):
        act = (gate * jax.nn.sigmoid(gate) * up).astype(BF16)
        x3 = x_ref[rows_of(i), :] + 0.5 * _dot(act, wd_ref[...])
        y_ref[rows_of(i), :] = _rms(x3, gfin_ref[...])

    gu = gate_up(0)
    for i in range(n_sub):
        nxt = gate_up(i + 1) if i + 1 < n_sub else None
        down(i, *gu)
        gu = nxt


def _resident(shape):
    return pl.BlockSpec(shape, lambda *_: (0,) * len(shape), pipeline_mode=pl.Buffered(1))


def _run_mixer(cfg, x, history, sinks, weights):
    batch, seq, _ = x.shape
    bb, ts, n_seq, n_tiles = cfg.bb, cfg.ts, cfg.n_seq, cfg.n_tiles
    assert batch % bb == 0 and seq == ts * n_seq and ts % CHUNK == 0
    assert n_tiles == (batch // bb) * n_seq
    assert ts >= WINDOW or 2 * ts == WINDOW

    def tile_a(t):
        return jnp.minimum(t, n_tiles - 1)

    def tile_b(t):
        return jnp.maximum(t - 1, 0)

    def per_batch_b(*tail):
        return pl.BlockSpec((bb,) + tail, lambda t: (tile_b(t) // n_seq,) + (0,) * len(tail))

    state_specs = [
        per_batch_b(WINDOW, KV_WIDTH), per_batch_b(WINDOW, KV_WIDTH),
        per_batch_b(N_LSTM_HEADS, LSTM_HEAD_DIM, LSTM_HEAD_DIM),
        per_batch_b(N_LSTM_HEADS, LSTM_HEAD_DIM), per_batch_b(1, LANES),
    ]
    in_specs = [
        pl.BlockSpec(memory_space=pltpu.SMEM),
        pl.BlockSpec((bb, ts, D_MODEL), lambda t: (tile_a(t) // n_seq, tile_a(t) % n_seq, 0)),
    ] + (state_specs if cfg.has_history else []) + [_resident(w.shape) for w in weights]
    out_shape = (
        jax.ShapeDtypeStruct((batch, seq, D_MODEL), F32),
        jax.ShapeDtypeStruct((batch, WINDOW, KV_WIDTH), F32),
        jax.ShapeDtypeStruct((batch, WINDOW, KV_WIDTH), F32),
        jax.ShapeDtypeStruct((batch, N_LSTM_HEADS, LSTM_HEAD_DIM, LSTM_HEAD_DIM), F32),
        jax.ShapeDtypeStruct((batch, N_LSTM_HEADS, LSTM_HEAD_DIM), F32),
        jax.ShapeDtypeStruct((batch, 1, LANES), F32),
    )
    out_specs = [
        pl.BlockSpec((bb, ts, D_MODEL), lambda t: (tile_b(t) // n_seq, tile_b(t) % n_seq, 0)),
    ] + state_specs
    rows = bb * ts
    hist = pltpu.VMEM((4, bb, WINDOW + ts, LANES), BF16)
    return pl.pallas_call(
        functools.partial(_mixer_kernel, cfg),
        grid=(n_tiles + 1,),
        in_specs=in_specs, out_specs=out_specs, out_shape=out_shape,
        scratch_shapes=[hist, hist,
                        pltpu.VMEM((2, rows, D_MODEL), F32),
                        pltpu.VMEM((2, rows, _W_IN_PADDED), F32)],
        compiler_params=pltpu.CompilerParams(
            dimension_semantics=("arbitrary",), vmem_limit_bytes=_VMEM_LIMIT),
        name=f"mixer_{'sample' if cfg.has_history else 'prompt'}",
    )(sinks, x, *history, *weights)


def _run_ffn(x, weights, name):
    n_rows = x.shape[0]
    assert n_rows % _FFN_ROWS == 0
    row_spec = pl.BlockSpec((_FFN_ROWS, D_MODEL), lambda i: (i, 0))
    return pl.pallas_call(
        _ffn_kernel,
        grid=(n_rows // _FFN_ROWS,),
        in_specs=[row_spec] + [_resident(w.shape) for w in weights],
        out_specs=row_spec,
        out_shape=jax.ShapeDtypeStruct((n_rows, D_MODEL), F32),
        compiler_params=pltpu.CompilerParams(
            dimension_semantics=("arbitrary",), vmem_limit_bytes=_VMEM_LIMIT),
        name=name,
    )(x, *weights)


def _pad_gate_lanes(t):
    out = jnp.zeros((t.shape[0], 1, LANES), F32)
    return out.at[:, 0, GATE_LANE0:GATE_LANE0 + N_LSTM_HEADS].set(t)


def kernel(x_prompt, x_sample, cache_k, cache_v, state_C, state_n, state_m, g_ffn1, w1_gate, w1_up, w1_down, g_mix, w_in, b_igate, b_fgate, g_q, g_k, attn_sinks, g_lstm_out, w_out, g_ffn2, w2_gate, w2_up, w2_down, g_final):
    depth = w_in.shape[0]
    assert depth == 1
    l = 0
    batch, seq, _ = x_prompt.shape
    dec_batch, dec_seq, _ = x_sample.shape
    row = lambda t: t.reshape(1, -1).astype(F32)
    gate_pad = jnp.zeros((D_MODEL, LANES - 2 * N_LSTM_HEADS), F32)
    win = jnp.concatenate([w_in[l], gate_pad], axis=1).astype(BF16)
    gbias = jnp.concatenate(
        [b_igate[l], b_fgate[l], jnp.zeros((LANES - 2 * N_LSTM_HEADS,), F32)]).reshape(1, LANES)
    mixer_weights = (
        row(g_ffn1[l]), w1_gate[l].astype(BF16), w1_up[l].astype(BF16), w1_down[l].astype(BF16),
        row(g_mix[l]), win, gbias,
        row(jnp.tile(g_q[l], N_ATT_HEADS)), row(jnp.tile(g_k[l], N_KV_HEADS)),
        row(g_lstm_out[l]), w_out[l].astype(BF16),
    )
    ffn_weights = (
        row(g_ffn2[l]), w2_gate[l].astype(BF16), w2_up[l].astype(BF16), w2_down[l].astype(BF16),
        row(g_final[l]),
    )
    sinks = attn_sinks[l].astype(F32)

    bb = 4
    prompt_cfg = _Tile(bb=bb, ts=CHUNK, n_seq=seq // CHUNK, n_tiles=(batch // bb) * (seq // CHUNK),
                       has_history=False)
    x2p, kp, vp, cp, n_p, mp = _run_mixer(prompt_cfg, x_prompt, (), sinks, mixer_weights)
    yp = _run_ffn(x2p.reshape(batch * seq, D_MODEL), ffn_weights, "ffn_prompt")

    rows = cache_k.shape[2]
    assert rows == WINDOW
    sample_cfg = _Tile(bb=bb, ts=dec_seq, n_seq=1, n_tiles=dec_batch // bb, has_history=True)
    history = (cache_k[l].reshape(dec_batch, rows, KV_WIDTH),
               cache_v[l].reshape(dec_batch, rows, KV_WIDTH),
               state_C[l], state_n[l], _pad_gate_lanes(state_m[l]))
    x2s, ks, vs, cs, n_s, ms = _run_mixer(sample_cfg, x_sample, history, sinks, mixer_weights)
    ys = _run_ffn(x2s.reshape(dec_batch * dec_seq, D_MODEL), ffn_weights, "ffn_sample")

    kv_shape = lambda b: (1, b, rows, N_KV_HEADS, HEAD_DIM)
    m_out = lambda t: t[:, 0, GATE_LANE0:GATE_LANE0 + N_LSTM_HEADS][None]
    return (yp.reshape(batch, seq, D_MODEL), ys.reshape(dec_batch, dec_seq, D_MODEL),
            kp.reshape(kv_shape(batch)), vp.reshape(kv_shape(batch)),
            cp[None], n_p[None], m_out(mp),
            ks.reshape(kv_shape(dec_batch)), vs.reshape(kv_shape(dec_batch)),
            cs[None], n_s[None], m_out(ms))
```

```python
import dataclasses
import functools

import jax
import jax.numpy as jnp
from jax import lax
from jax.experimental import pallas as pl
from jax.experimental.pallas import tpu as pltpu

F32 = jnp.float32
BF16 = jnp.bfloat16

D_MODEL = 1024
D_FF = 2816
CHUNK = 64
WINDOW = 128
N_KEYS = WINDOW + CHUNK
N_ATT_HEADS = 8
HEAD_DIM = 64
N_KV_HEADS = 2
GQA_GROUP = N_ATT_HEADS // N_KV_HEADS
ATT_WIDTH = N_ATT_HEADS * HEAD_DIM
KV_WIDTH = N_KV_HEADS * HEAD_DIM
N_LSTM_HEADS = 4
LSTM_HEAD_DIM = 128
LSTM_WIDTH = N_LSTM_HEADS * LSTM_HEAD_DIM
EPS = 1e-6
LANES = 128
GATE_LANE0 = 4

_OFF_Q = 0
_OFF_K = _OFF_Q + ATT_WIDTH
_OFF_V = _OFF_K + KV_WIDTH
_OFF_MQ = _OFF_V + KV_WIDTH
_OFF_MK = _OFF_MQ + LSTM_WIDTH
_OFF_GATE = _OFF_MK + LSTM_WIDTH
_W_IN_EARLY = _OFF_GATE + LANES
_OFF_MV = 0
_OFF_MO = _OFF_MV + LSTM_WIDTH
_W_IN_LATE = _OFF_MO + LSTM_WIDTH

_V7X_VMEM_BYTES = 64 * 1024 * 1024
_VMEM_LIMIT = _V7X_VMEM_BYTES - 2 * 1024 * 1024
_FFN_ROWS = 1024
_FFN_SUB_ROWS = 256
_FFN_BLOCK = 512
_FFN_SPLITS = tuple((c, min(c + _FFN_BLOCK, D_FF)) for c in range(0, D_FF, _FFN_BLOCK))
_A_BLOCKS_PER_PHASE = (2, 2, 1, 1)
_IN_PROJ_SPLITS = ((0, 1024), (1024, _W_IN_EARLY))
_NT = (((1,), (1,)), ((), ()))
_TN = (((0,), (0,)), ((), ()))


@dataclasses.dataclass(frozen=True)
class _Tile:
    bb: int
    ts: int
    n_seq: int
    n_tiles: int
    has_history: bool


def _dot(a, b):
    return jnp.dot(a, b, preferred_element_type=F32)


def _rms(x, g):
    ms = jnp.mean(x * x, axis=-1, keepdims=True)
    return x * lax.rsqrt(ms + EPS) * g


def _rms_head_pairs(x, g, lo):
    outs = []
    for j in range(x.shape[-1] // LANES):
        t = x[:, j * LANES:(j + 1) * LANES]
        y = t * t
        s_lo = jnp.sum(jnp.where(lo, y, 0.0), axis=-1, keepdims=True)
        s_hi = jnp.sum(jnp.where(lo, 0.0, y), axis=-1, keepdims=True)
        inv = jnp.where(lo, lax.rsqrt(s_lo * (1.0 / HEAD_DIM) + EPS),
                        lax.rsqrt(s_hi * (1.0 / HEAD_DIM) + EPS))
        outs.append(t * inv * g[:, j * LANES:(j + 1) * LANES])
    return jnp.concatenate(outs, axis=-1)


def _lane_column(x, sel):
    return jnp.sum(jnp.where(sel, x, 0.0), axis=1, keepdims=True)


def _kv_variants(t, lo):
    r = pltpu.roll(t, HEAD_DIM, axis=1)
    z = jnp.zeros_like(t)
    return (jnp.where(lo, t, z).astype(BF16), jnp.where(lo, z, r).astype(BF16),
            jnp.where(lo, r, z).astype(BF16), jnp.where(lo, z, t).astype(BF16))


def _cumsum_chunks(x, tri):
    hi = x.astype(BF16)
    r1 = x - hi.astype(F32)
    mid = r1.astype(BF16)
    lo = (r1 - mid.astype(F32)).astype(BF16)
    return _dot(tri, hi) + _dot(tri, mid) + _dot(tri, lo)


def _mixer_kernel(cfg, *refs):
    it = iter(refs)
    sink_ref, x_ref = next(it), next(it)
    if cfg.has_history:
        ck_ref, cv_ref, c0_ref, n0_ref, m0_ref = (next(it) for _ in range(5))
    (g1_ref, w1g_ref, w1u_ref, w1d_ref, gmix_ref, win_ref, gbias_ref,
     wlate_ref, gq_ref, gk_ref, glo_ref, wo_ref) = (next(it) for _ in range(12))
    x2_ref, ko_ref, vo_ref, c_ref, n_ref, m_ref = (next(it) for _ in range(6))
    kh_ref, vh_ref, x1s_ref, zs_ref, h2s_ref = (next(it) for _ in range(5))

    bb, ts = cfg.bb, cfg.ts
    rows = bb * ts
    n_chunks = ts // CHUNK
    step = pl.program_id(0)
    slot_a = lax.rem(step, 2)
    slot_b = 1 - slot_a
    s_idx = lax.rem(jnp.maximum(step - 1, 0), cfg.n_seq)
    neg_inf = F32(-jnp.inf)

    lane = lax.broadcasted_iota(jnp.int32, (1, LANES), 1)
    lo = lane < HEAD_DIM

    @pl.when(step == 0)
    def _():
        x1s_ref[1] = jnp.zeros((rows, D_MODEL), F32)
        zs_ref[1] = jnp.zeros((rows, _W_IN_EARLY), F32)
        h2s_ref[1] = jnp.zeros((rows, D_MODEL), BF16)

    @pl.when(s_idx == 0)
    def _():
        for b in range(bb):
            if cfg.has_history:
                kvar = _kv_variants(ck_ref[b], lo)
                vvar = _kv_variants(cv_ref[b], lo)
                for i in range(4):
                    kh_ref[i, b, 0:WINDOW, :] = kvar[i]
                    vh_ref[i, b, 0:WINDOW, :] = vvar[i]
                ko_ref[b] = ck_ref[b]
                vo_ref[b] = cv_ref[b]
                c_ref[b] = c0_ref[b]
                n_ref[b] = n0_ref[b]
                m_ref[b] = m0_ref[b]
            else:
                zero = jnp.zeros((WINDOW, LANES), BF16)
                for i in range(4):
                    kh_ref[i, b, 0:WINDOW, :] = zero
                    vh_ref[i, b, 0:WINDOW, :] = zero
                ko_ref[b] = jnp.zeros((WINDOW, KV_WIDTH), F32)
                vo_ref[b] = jnp.zeros((WINDOW, KV_WIDTH), F32)
                c_ref[b] = jnp.zeros((N_LSTM_HEADS, LSTM_HEAD_DIM, LSTM_HEAD_DIM), F32)
                n_ref[b] = jnp.zeros((N_LSTM_HEADS, LSTM_HEAD_DIM), F32)
                m_ref[b] = jnp.zeros((1, LANES), F32)

    xa = x_ref[...].reshape(rows, D_MODEL)
    ha = _rms(xa, g1_ref[...]).astype(BF16)
    ffn_acc = []

    pending_act = []

    def stage_a_ffn(lo_col, hi_col):
        gate = _dot(ha, w1g_ref[:, lo_col:hi_col])
        up = _dot(ha, w1u_ref[:, lo_col:hi_col])
        stage_a_down()
        pending_act.append(((gate * jax.nn.sigmoid(gate) * up).astype(BF16), lo_col, hi_col))

    def stage_a_down():
        if pending_act:
            act, lo_col, hi_col = pending_act.pop()
            ffn_acc.append(_dot(act, w1d_ref[lo_col:hi_col, :]))

    def stage_a_in_proj():
        stage_a_down()
        y = ffn_acc[0]
        for part in ffn_acc[1:]:
            y = y + part
        x1a = xa + 0.5 * y
        x1s_ref[slot_a] = x1a
        h2a = _rms(x1a, gmix_ref[...]).astype(BF16)
        h2s_ref[slot_a] = h2a
        for lo_col, hi_col in _IN_PROJ_SPLITS:
            zs_ref[slot_a, :, lo_col:hi_col] = _dot(h2a, win_ref[:, lo_col:hi_col])

    a_blocks = iter(_FFN_SPLITS)

    def emit_a(n):
        for _ in range(n):
            stage_a_ffn(*next(a_blocks))

    def proj(off, width):
        return zs_ref[slot_b, :, off:off + width]

    h2b = h2s_ref[slot_b]
    mvb = _dot(h2b, wlate_ref[:, _OFF_MV:_OFF_MV + LSTM_WIDTH]).astype(BF16)
    mo = _dot(h2b, wlate_ref[:, _OFF_MO:_OFF_MO + LSTM_WIDTH])
    x1 = x1s_ref[slot_b]
    q = _rms_head_pairs(proj(_OFF_Q, ATT_WIDTH), gq_ref[...], lo)
    qb = (q * (HEAD_DIM ** -0.5)).astype(BF16)
    kn = _rms_head_pairs(proj(_OFF_K, KV_WIDTH), gk_ref[...], lo)
    v = proj(_OFF_V, KV_WIDTH)
    mq = proj(_OFF_MQ, LSTM_WIDTH)
    mqb = mq.astype(BF16)
    mk = proj(_OFF_MK, LSTM_WIDTH) * (LSTM_HEAD_DIM ** -0.5)
    mkb = mk.astype(BF16)
    gates = proj(_OFF_GATE, LANES) + gbias_ref[...]

    log_f = jnp.minimum(gates, 0.0) - jnp.log1p(jnp.exp(-jnp.abs(gates)))
    log_i = pltpu.roll(gates, GATE_LANE0, axis=1)
    ri = lax.broadcasted_iota(jnp.int32, (rows, rows), 0)
    ci = lax.broadcasted_iota(jnp.int32, (rows, rows), 1)
    tri = ((ci <= ri) & ((ci // CHUNK) == (ri // CHUNK))).astype(BF16)
    b_all = _cumsum_chunks(log_f, tri)
    emit_a(_A_BLOCKS_PER_PHASE[0])
    r_all = log_i - b_all
    r_rows = r_all.T

    qi = lax.broadcasted_iota(jnp.int32, (CHUNK, N_KEYS), 0)
    kj = lax.broadcasted_iota(jnp.int32, (CHUNK, N_KEYS), 1)
    dist = jnp.abs(WINDOW + qi - kj).astype(F32)
    half = lax.broadcasted_iota(jnp.int32, (2 * CHUNK, 1), 0) < CHUNK
    alibi, sink_col = [], []
    for kh in range(N_KV_HEADS):
        for pos in range(2):
            h_a = GQA_GROUP * kh + pos
            h_b = h_a + 2
            alibi.append(jnp.concatenate(
                [dist * (2.0 ** -(h_a + 1)), dist * (2.0 ** -(h_b + 1))], axis=0))
            sink_col.append(jnp.where(half, sink_ref[h_a], sink_ref[h_b]))
    key_idx = lax.broadcasted_iota(jnp.int32, (1, N_KEYS), 1)
    ti = lax.broadcasted_iota(jnp.int32, (CHUNK, CHUNK), 0)
    si = lax.broadcasted_iota(jnp.int32, (CHUNK, CHUNK), 1)
    causal = si <= ti

    assert n_chunks == 1, "stage B is written phase-major for one chunk per batch row"
    row0 = [b * ts for b in range(bb)]
    att_jobs = [(b, kh, pos) for b in range(bb) for kh in range(N_KV_HEADS) for pos in range(2)]
    jobs = [(b, h) for b in range(bb) for h in range(N_LSTM_HEADS)]
    hsl = lambda h: slice(h * LSTM_HEAD_DIM, (h + 1) * LSTM_HEAD_DIM)
    rsl = lambda b: slice(row0[b], row0[b] + CHUNK)

    for b in range(bb):
        kvar = _kv_variants(kn[row0[b]:row0[b] + ts], lo)
        vvar = _kv_variants(v[row0[b]:row0[b] + ts], lo)
        for i in range(4):
            kh_ref[i, b, WINDOW:WINDOW + ts, :] = kvar[i]
            vh_ref[i, b, WINDOW:WINDOW + ts, :] = vvar[i]
    scores = []
    for b, kh, pos in att_jobs:
        rs = row0[b]
        qs = jnp.concatenate(
            [qb[rs:rs + CHUNK, (2 * kh) * LANES:(2 * kh + 1) * LANES],
             qb[rs:rs + CHUNK, (2 * kh + 1) * LANES:(2 * kh + 2) * LANES]], axis=0)
        keys = kh_ref[2 * kh + pos, b, 0:N_KEYS, :]
        scores.append(lax.dot_general(qs, keys, _NT, preferred_element_type=F32))
    c_old = {j: c_ref[j[0], j[1]] for j in jobs}
    n_old = {j: n_ref[j[0], j[1]:j[1] + 1, :] for j in jobs}
    qk = {(b, h): lax.dot_general(mqb[rsl(b), hsl(h)], mkb[rsl(b), hsl(h)], _NT,
                                  preferred_element_type=F32) for b, h in jobs}
    q_c = {(b, h): _dot(mqb[rsl(b), hsl(h)], c_old[(b, h)].astype(BF16)) for b, h in jobs}
    emit_a(_A_BLOCKS_PER_PHASE[1])

    if not cfg.has_history:
        first_valid = (WINDOW // CHUNK - s_idx) * CHUNK
        alibi = [jnp.where(key_idx >= first_valid, al, -neg_inf) for al in alibi]
    probs, dens = [], []
    for (b, kh, pos), sc in zip(att_jobs, scores):
        i = 2 * kh + pos
        sc = sc - alibi[i]
        mx = jnp.maximum(jnp.max(sc, axis=-1, keepdims=True), sink_col[i])
        p = jnp.exp(sc - mx)
        dens.append(jnp.sum(p, axis=-1, keepdims=True) + jnp.exp(sink_col[i] - mx))
        probs.append(p.astype(BF16))
    m_rows = [m_ref[b] for b in range(bb)]
    bcs = [b_all[rsl(b)] for b in range(bb)]
    rcs = [r_all[rsl(b)] for b in range(bb)]
    inters, decays, wss, m_news = [], [], [], []
    for b in range(bb):
        tot = bcs[b][CHUNK - 1:CHUNK, :]
        g = tot + rcs[b]
        m_new = jnp.maximum(tot + m_rows[b], jnp.max(g, axis=0, keepdims=True))
        inters.append(bcs[b] + m_rows[b])
        decays.append(jnp.exp(tot + m_rows[b] - m_new))
        wss.append(jnp.exp(g - m_new))
        m_news.append(m_new)
    a_mat, mts, w_inters = {}, {}, {}
    for b, h in jobs:
        sel = lane == GATE_LANE0 + h
        b_col = _lane_column(bcs[b], sel)
        inter_col = _lane_column(inters[b], sel)
        r_row = r_rows[GATE_LANE0 + h:GATE_LANE0 + h + 1, rsl(b)]
        d = jnp.where(causal, b_col + r_row, neg_inf)
        mt = jnp.maximum(inter_col, jnp.max(d, axis=1, keepdims=True))
        a_mat[(b, h)] = jnp.exp(d - mt) * qk[(b, h)]
        mts[(b, h)] = mt
        w_inters[(b, h)] = jnp.exp(inter_col - mt)
    wks = {}
    for b, h in jobs:
        wks[(b, h)] = _lane_column(wss[b], lane == GATE_LANE0 + h) * mk[rsl(b), hsl(h)]

    att_o = [_dot(p, vh_ref[2 * kh + pos, b, 0:N_KEYS, :])
             for (b, kh, pos), p in zip(att_jobs, probs)]
    intra = {(b, h): _dot(a_mat[(b, h)].astype(BF16), mvb[rsl(b), hsl(h)]) for b, h in jobs}
    c_upd = {(b, h): lax.dot_general(wks[(b, h)].astype(BF16), mvb[rsl(b), hsl(h)], _TN,
                                     preferred_element_type=F32) for b, h in jobs}
    emit_a(_A_BLOCKS_PER_PHASE[2])

    pair_out = [[None] * 4 for _ in range(bb)]
    for (b, kh, pos), o, den in zip(att_jobs, att_o, dens):
        o = o / den
        for part in range(2):
            j = 2 * kh + part
            piece = o[part * CHUNK:(part + 1) * CHUNK]
            pair_out[b][j] = piece if pair_out[b][j] is None else pair_out[b][j] + piece
    att_rows = [jnp.concatenate(pair_out[b], axis=-1) for b in range(bb)]
    head_out = {}
    for b, h in jobs:
        j = (b, h)
        q_n = jnp.sum(mq[rsl(b), hsl(h)] * n_old[j], axis=1, keepdims=True)
        num = intra[j] + w_inters[j] * q_c[j]
        den = jnp.sum(a_mat[j], axis=1, keepdims=True) + w_inters[j] * q_n
        head_out[j] = num / jnp.maximum(jnp.abs(den), jnp.exp(-mts[j]))
    for b, h in jobs:
        j = (b, h)
        dec = _lane_column(decays[b], lane == GATE_LANE0 + h)
        c_ref[b, h] = dec * c_old[j] + c_upd[j]
        n_ref[b, h:h + 1, :] = dec * n_old[j] + jnp.sum(wks[j], axis=0, keepdims=True)
    hl_rows = [jnp.concatenate([head_out[(b, h)] for h in range(N_LSTM_HEADS)], axis=-1)
               for b in range(bb)]
    for b in range(bb):
        r0 = row0[b]
        m_ref[b] = m_news[b]
        if cfg.n_seq > 1:
            for i in range(4):
                kh_ref[i, b, 0:WINDOW, :] = kh_ref[i, b, ts:ts + WINDOW, :]
                vh_ref[i, b, 0:WINDOW, :] = vh_ref[i, b, ts:ts + WINDOW, :]
        ko_ref[b, 0:WINDOW - ts, :] = ko_ref[b, ts:WINDOW, :]
        vo_ref[b, 0:WINDOW - ts, :] = vo_ref[b, ts:WINDOW, :]
        ko_ref[b, WINDOW - ts:WINDOW, :] = kn[r0:r0 + ts]
        vo_ref[b, WINDOW - ts:WINDOW, :] = v[r0:r0 + ts]
    emit_a(_A_BLOCKS_PER_PHASE[3])

    att = jnp.concatenate(att_rows, axis=0).astype(BF16)
    hm = jnp.concatenate(hl_rows, axis=0)
    gated = jax.nn.sigmoid(mo) * hm
    glo = glo_ref[...]
    hl = jnp.concatenate(
        [_rms(gated[:, hsl(h)], glo[:, hsl(h)]) for h in range(N_LSTM_HEADS)],
        axis=-1).astype(BF16)
    mixed = jnp.concatenate([att, hl], axis=-1)
    x2_ref[...] = (x1 + _dot(mixed, wo_ref[...])).reshape(bb, ts, D_MODEL)

    stage_a_in_proj()
    assert next(a_blocks, None) is None


def _ffn_kernel(x_ref, g_ref, wg_ref, wu_ref, wd_ref, gfin_ref, y_ref):
    n_sub = _FFN_ROWS // _FFN_SUB_ROWS
    rows_of = lambda i: slice(i * _FFN_SUB_ROWS, (i + 1) * _FFN_SUB_ROWS)

    def gate_up(i):
        h = _rms(x_ref[rows_of(i), :], g_ref[...]).astype(BF16)
        gate = _dot(h, wg_ref[...])
        up = _dot(h, wu_ref[...])
        return gate, up

    def down(i, gate, up):
        act = (gate * jax.nn.sigmoid(gate) * up).astype(BF16)
        x3 = x_ref[rows_of(i), :] + 0.5 * _dot(act, wd_ref[...])
        y_ref[rows_of(i), :] = _rms(x3, gfin_ref[...])

    gu = gate_up(0)
    for i in range(n_sub):
        nxt = gate_up(i + 1) if i + 1 < n_sub else None
        down(i, *gu)
        gu = nxt


def _resident(shape):
    return pl.BlockSpec(shape, lambda *_: (0,) * len(shape), pipeline_mode=pl.Buffered(1))


def _run_mixer(cfg, x, history, sinks, weights):
    batch, seq, _ = x.shape
    bb, ts, n_seq, n_tiles = cfg.bb, cfg.ts, cfg.n_seq, cfg.n_tiles
    assert batch % bb == 0 and seq == ts * n_seq and ts % CHUNK == 0
    assert n_tiles == (batch // bb) * n_seq
    assert ts >= WINDOW or 2 * ts == WINDOW

    def tile_a(t):
        return jnp.minimum(t, n_tiles - 1)

    def tile_b(t):
        return jnp.maximum(t - 1, 0)

    def per_batch_b(*tail):
        return pl.BlockSpec((bb,) + tail, lambda t: (tile_b(t) // n_seq,) + (0,) * len(tail))

    state_specs = [
        per_batch_b(WINDOW, KV_WIDTH), per_batch_b(WINDOW, KV_WIDTH),
        per_batch_b(N_LSTM_HEADS, LSTM_HEAD_DIM, LSTM_HEAD_DIM),
        per_batch_b(N_LSTM_HEADS, LSTM_HEAD_DIM), per_batch_b(1, LANES),
    ]
    in_specs = [
        pl.BlockSpec(memory_space=pltpu.SMEM),
        pl.BlockSpec((bb, ts, D_MODEL), lambda t: (tile_a(t) // n_seq, tile_a(t) % n_seq, 0)),
    ] + (state_specs if cfg.has_history else []) + [_resident(w.shape) for w in weights]
    out_shape = (
        jax.ShapeDtypeStruct((batch, seq, D_MODEL), F32),
        jax.ShapeDtypeStruct((batch, WINDOW, KV_WIDTH), F32),
        jax.ShapeDtypeStruct((batch, WINDOW, KV_WIDTH), F32),
        jax.ShapeDtypeStruct((batch, N_LSTM_HEADS, LSTM_HEAD_DIM, LSTM_HEAD_DIM), F32),
        jax.ShapeDtypeStruct((batch, N_LSTM_HEADS, LSTM_HEAD_DIM), F32),
        jax.ShapeDtypeStruct((batch, 1, LANES), F32),
    )
    out_specs = [
        pl.BlockSpec((bb, ts, D_MODEL), lambda t: (tile_b(t) // n_seq, tile_b(t) % n_seq, 0)),
    ] + state_specs
    rows = bb * ts
    hist = pltpu.VMEM((4, bb, WINDOW + ts, LANES), BF16)
    return pl.pallas_call(
        functools.partial(_mixer_kernel, cfg),
        grid=(n_tiles + 1,),
        in_specs=in_specs, out_specs=out_specs, out_shape=out_shape,
        scratch_shapes=[hist, hist,
                        pltpu.VMEM((2, rows, D_MODEL), F32),
                        pltpu.VMEM((2, rows, _W_IN_EARLY), F32),
                        pltpu.VMEM((2, rows, D_MODEL), BF16)],
        compiler_params=pltpu.CompilerParams(
            dimension_semantics=("arbitrary",), vmem_limit_bytes=_VMEM_LIMIT),
        name=f"mixer_{'sample' if cfg.has_history else 'prompt'}",
    )(sinks, x, *history, *weights)


def _run_ffn(x, weights, name):
    n_rows = x.shape[0]
    assert n_rows % _FFN_ROWS == 0
    row_spec = pl.BlockSpec((_FFN_ROWS, D_MODEL), lambda i: (i, 0))
    return pl.pallas_call(
        _ffn_kernel,
        grid=(n_rows // _FFN_ROWS,),
        in_specs=[row_spec] + [_resident(w.shape) for w in weights],
        out_specs=row_spec,
        out_shape=jax.ShapeDtypeStruct((n_rows, D_MODEL), F32),
        compiler_params=pltpu.CompilerParams(
            dimension_semantics=("arbitrary",), vmem_limit_bytes=_VMEM_LIMIT),
        name=name,
    )(x, *weights)


def _pad_gate_lanes(t):
    out = jnp.zeros((t.shape[0], 1, LANES), F32)
    return out.at[:, 0, GATE_LANE0:GATE_LANE0 + N_LSTM_HEADS].set(t)


def kernel(x_prompt, x_sample, cache_k, cache_v, state_C, state_n, state_m, g_ffn1, w1_gate, w1_up, w1_down, g_mix, w_in, b_igate, b_fgate, g_q, g_k, attn_sinks, g_lstm_out, w_out, g_ffn2, w2_gate, w2_up, w2_down, g_final):
    depth = w_in.shape[0]
    assert depth == 1
    l = 0
    batch, seq, _ = x_prompt.shape
    dec_batch, dec_seq, _ = x_sample.shape
    row = lambda t: t.reshape(1, -1).astype(F32)
    gate_pad = jnp.zeros((D_MODEL, LANES - 2 * N_LSTM_HEADS), F32)
    n_early = _OFF_GATE
    n_late = _W_IN_LATE
    win = jnp.concatenate(
        [w_in[l][:, :n_early], w_in[l][:, n_early + n_late:], gate_pad], axis=1).astype(BF16)
    wlate = w_in[l][:, n_early:n_early + n_late].astype(BF16)
    gbias = jnp.concatenate(
        [b_igate[l], b_fgate[l], jnp.zeros((LANES - 2 * N_LSTM_HEADS,), F32)]).reshape(1, LANES)
    mixer_weights = (
        row(g_ffn1[l]), w1_gate[l].astype(BF16), w1_up[l].astype(BF16), w1_down[l].astype(BF16),
        row(g_mix[l]), win, gbias, wlate,
        row(jnp.tile(g_q[l], N_ATT_HEADS)), row(jnp.tile(g_k[l], N_KV_HEADS)),
        row(g_lstm_out[l]), w_out[l].astype(BF16),
    )
    ffn_weights = (
        row(g_ffn2[l]), w2_gate[l].astype(BF16), w2_up[l].astype(BF16), w2_down[l].astype(BF16),
        row(g_final[l]),
    )
    sinks = attn_sinks[l].astype(F32)

    bb = 4
    prompt_cfg = _Tile(bb=bb, ts=CHUNK, n_seq=seq // CHUNK, n_tiles=(batch // bb) * (seq // CHUNK),
                       has_history=False)
    x2p, kp, vp, cp, n_p, mp = _run_mixer(prompt_cfg, x_prompt, (), sinks, mixer_weights)
    yp = _run_ffn(x2p.reshape(batch * seq, D_MODEL), ffn_weights, "ffn_prompt")

    rows = cache_k.shape[2]
    assert rows == WINDOW
    sample_cfg = _Tile(bb=bb, ts=dec_seq, n_seq=1, n_tiles=dec_batch // bb, has_history=True)
    history = (cache_k[l].reshape(dec_batch, rows, KV_WIDTH),
               cache_v[l].reshape(dec_batch, rows, KV_WIDTH),
               state_C[l], state_n[l], _pad_gate_lanes(state_m[l]))
    x2s, ks, vs, cs, n_s, ms = _run_mixer(sample_cfg, x_sample, history, sinks, mixer_weights)
    ys = _run_ffn(x2s.reshape(dec_batch * dec_seq, D_MODEL), ffn_weights, "ffn_sample")

    kv_shape = lambda b: (1, b, rows, N_KV_HEADS, HEAD_DIM)
    m_out = lambda t: t[:, 0, GATE_LANE0:GATE_LANE0 + N_LSTM_HEADS][None]
    return (yp.reshape(batch, seq, D_MODEL), ys.reshape(dec_batch, dec_seq, D_MODEL),
            kp.reshape(kv_shape(batch)), vp.reshape(kv_shape(batch)),
            cp[None], n_p[None], m_out(mp),
            ks.reshape(kv_shape(dec_batch)), vs.reshape(kv_shape(dec_batch)),
            cs[None], n_s[None], m_out(ms))
```

```python
import dataclasses
import functools

import jax
import jax.numpy as jnp
from jax import lax
from jax.experimental import pallas as pl
from jax.experimental.pallas import tpu as pltpu

F32 = jnp.float32
BF16 = jnp.bfloat16

D_MODEL = 1024
D_FF = 2816
CHUNK = 64
WINDOW = 128
N_KEYS = WINDOW + CHUNK
N_ATT_HEADS = 8
HEAD_DIM = 64
N_KV_HEADS = 2
GQA_GROUP = N_ATT_HEADS // N_KV_HEADS
ATT_WIDTH = N_ATT_HEADS * HEAD_DIM
KV_WIDTH = N_KV_HEADS * HEAD_DIM
N_LSTM_HEADS = 4
LSTM_HEAD_DIM = 128
LSTM_WIDTH = N_LSTM_HEADS * LSTM_HEAD_DIM
EPS = 1e-6
LANES = 128
GATE_LANE0 = 4

_OFF_Q = 0
_OFF_K = _OFF_Q + ATT_WIDTH
_OFF_V = _OFF_K + KV_WIDTH
_OFF_MQ = _OFF_V + KV_WIDTH
_OFF_MK = _OFF_MQ + LSTM_WIDTH
_OFF_MV = _OFF_MK + LSTM_WIDTH
_OFF_MO = _OFF_MV + LSTM_WIDTH
_OFF_GATE = _OFF_MO + LSTM_WIDTH
_W_IN_PADDED = _OFF_GATE + LANES

_V7X_VMEM_BYTES = 64 * 1024 * 1024
_VMEM_LIMIT = _V7X_VMEM_BYTES - 512 * 1024
_PROMPT_BATCH_ROWS = 8
_SAMPLE_BATCH_ROWS = 4
_FFN_ROWS = 1024
_FFN_SUB_ROWS = 256
_FFN_BLOCK = 512
_FFN_SPLITS = tuple((c, min(c + _FFN_BLOCK, D_FF)) for c in range(0, D_FF, _FFN_BLOCK))
_A_BLOCKS_PER_PHASE = (2, 2, 1, 1)
_IN_PROJ_SPLITS = ((0, 1024), (1024, 2048), (2048, _OFF_GATE))
_NT = (((1,), (1,)), ((), ()))
_TN = (((0,), (0,)), ((), ()))


@dataclasses.dataclass(frozen=True)
class _Tile:
    bb: int
    ts: int
    n_seq: int
    n_tiles: int
    has_history: bool


def _dot(a, b):
    return jnp.dot(a, b, preferred_element_type=F32)


def _rms(x, g):
    ms = jnp.mean(x * x, axis=-1, keepdims=True)
    return x * lax.rsqrt(ms + EPS) * g


def _rms_head_pairs(x, g, lo):
    outs = []
    for j in range(x.shape[-1] // LANES):
        t = x[:, j * LANES:(j + 1) * LANES]
        y = t * t
        s_lo = jnp.sum(jnp.where(lo, y, 0.0), axis=-1, keepdims=True)
        s_hi = jnp.sum(jnp.where(lo, 0.0, y), axis=-1, keepdims=True)
        inv = jnp.where(lo, lax.rsqrt(s_lo * (1.0 / HEAD_DIM) + EPS),
                        lax.rsqrt(s_hi * (1.0 / HEAD_DIM) + EPS))
        outs.append(t * inv * g[:, j * LANES:(j + 1) * LANES])
    return jnp.concatenate(outs, axis=-1)


def _lane_column(x, sel):
    return jnp.sum(jnp.where(sel, x, 0.0), axis=1, keepdims=True)


def _kv_variants(t, lo):
    r = pltpu.roll(t, HEAD_DIM, axis=1)
    z = jnp.zeros_like(t)
    return (jnp.where(lo, t, z).astype(BF16), jnp.where(lo, z, r).astype(BF16),
            jnp.where(lo, r, z).astype(BF16), jnp.where(lo, z, t).astype(BF16))


def _cumsum_chunks(x, tri):
    hi = x.astype(BF16)
    r1 = x - hi.astype(F32)
    mid = r1.astype(BF16)
    lo = (r1 - mid.astype(F32)).astype(BF16)
    return _dot(tri, hi) + _dot(tri, mid) + _dot(tri, lo)


def _mixer_kernel(cfg, *refs):
    it = iter(refs)
    sink_ref, x_ref = next(it), next(it)
    if cfg.has_history:
        ck_ref, cv_ref, c0_ref, n0_ref, m0_ref = (next(it) for _ in range(5))
    (g1_ref, w1g_ref, w1u_ref, w1d_ref, gmix_ref, win_ref, wgate_ref, gbias_ref,
     gq_ref, gk_ref, glo_ref, wo_ref) = (next(it) for _ in range(12))
    x2_ref, ko_ref, vo_ref, c_ref, n_ref, m_ref = (next(it) for _ in range(6))
    kh_ref, vh_ref, x1s_ref, zs_ref = (next(it) for _ in range(4))

    bb, ts = cfg.bb, cfg.ts
    rows = bb * ts
    n_chunks = ts // CHUNK
    step = pl.program_id(0)
    slot_a = 0
    slot_b = 0
    s_idx = lax.rem(jnp.maximum(step - 1, 0), cfg.n_seq)
    neg_inf = F32(-jnp.inf)

    lane = lax.broadcasted_iota(jnp.int32, (1, LANES), 1)
    lo = lane < HEAD_DIM

    @pl.when(step == 0)
    def _():
        x1s_ref[0] = jnp.zeros((rows, D_MODEL), F32)
        zs_ref[0] = jnp.zeros((rows, _W_IN_PADDED), F32)

    @pl.when(s_idx == 0)
    def _():
        for b in range(bb):
            if cfg.has_history:
                kvar = _kv_variants(ck_ref[b], lo)
                vvar = _kv_variants(cv_ref[b], lo)
                for i in range(4):
                    kh_ref[i, b, 0:WINDOW, :] = kvar[i]
                    vh_ref[i, b, 0:WINDOW, :] = vvar[i]
                ko_ref[b] = ck_ref[b]
                vo_ref[b] = cv_ref[b]
                c_ref[b] = c0_ref[b]
                n_ref[b] = n0_ref[b]
                m_ref[b] = m0_ref[b]
            else:
                zero = jnp.zeros((WINDOW, LANES), BF16)
                for i in range(4):
                    kh_ref[i, b, 0:WINDOW, :] = zero
                    vh_ref[i, b, 0:WINDOW, :] = zero
                ko_ref[b] = jnp.zeros((WINDOW, KV_WIDTH), F32)
                vo_ref[b] = jnp.zeros((WINDOW, KV_WIDTH), F32)
                c_ref[b] = jnp.zeros((N_LSTM_HEADS, LSTM_HEAD_DIM, LSTM_HEAD_DIM), F32)
                n_ref[b] = jnp.zeros((N_LSTM_HEADS, LSTM_HEAD_DIM), F32)
                m_ref[b] = jnp.zeros((1, LANES), F32)

    xa = x_ref[...].reshape(rows, D_MODEL)
    ha = _rms(xa, g1_ref[...]).astype(BF16)
    ffn_acc = []
    pending_act = []

    def stage_a_ffn(lo_col, hi_col):
        gate = _dot(ha, w1g_ref[:, lo_col:hi_col])
        up = _dot(ha, w1u_ref[:, lo_col:hi_col])
        stage_a_down()
        pending_act.append(((gate * jax.nn.sigmoid(gate) * up).astype(BF16), lo_col, hi_col))

    def stage_a_down():
        if pending_act:
            act, lo_col, hi_col = pending_act.pop()
            ffn_acc.append(_dot(act, w1d_ref[lo_col:hi_col, :]))

    def stage_a_in_proj():
        stage_a_down()
        y = ffn_acc[0]
        for part in ffn_acc[1:]:
            y = y + part
        x1a = xa + 0.5 * y
        x1s_ref[slot_a] = x1a
        h2a = _rms(x1a, gmix_ref[...]).astype(BF16)
        for lo_col, hi_col in _IN_PROJ_SPLITS:
            zs_ref[slot_a, :, lo_col:hi_col] = _dot(h2a, win_ref[:, lo_col:hi_col])
        zs_ref[slot_a, :, _OFF_GATE:_W_IN_PADDED] = _dot(h2a, wgate_ref[...])

    a_blocks = iter(_FFN_SPLITS)

    def emit_a(n):
        for _ in range(n):
            stage_a_ffn(*next(a_blocks))

    def proj(off, width):
        return zs_ref[slot_b, :, off:off + width]

    x1 = x1s_ref[slot_b]
    q = _rms_head_pairs(proj(_OFF_Q, ATT_WIDTH), gq_ref[...], lo)
    qb = (q * (HEAD_DIM ** -0.5)).astype(BF16)
    kn = _rms_head_pairs(proj(_OFF_K, KV_WIDTH), gk_ref[...], lo)
    v = proj(_OFF_V, KV_WIDTH)
    mq = proj(_OFF_MQ, LSTM_WIDTH)
    mqb = mq.astype(BF16)
    mk = proj(_OFF_MK, LSTM_WIDTH) * (LSTM_HEAD_DIM ** -0.5)
    mkb = mk.astype(BF16)
    mvb = proj(_OFF_MV, LSTM_WIDTH).astype(BF16)
    mo = proj(_OFF_MO, LSTM_WIDTH)
    gates = proj(_OFF_GATE, LANES) + gbias_ref[...]

    log_f = jnp.minimum(gates, 0.0) - jnp.log1p(jnp.exp(-jnp.abs(gates)))
    log_i = pltpu.roll(gates, GATE_LANE0, axis=1)
    ri = lax.broadcasted_iota(jnp.int32, (rows, rows), 0)
    ci = lax.broadcasted_iota(jnp.int32, (rows, rows), 1)
    tri = ((ci <= ri) & ((ci // CHUNK) == (ri // CHUNK))).astype(BF16)
    b_all = _cumsum_chunks(log_f, tri)
    emit_a(_A_BLOCKS_PER_PHASE[0])
    r_all = log_i - b_all
    r_rows = r_all.T

    qi = lax.broadcasted_iota(jnp.int32, (CHUNK, N_KEYS), 0)
    kj = lax.broadcasted_iota(jnp.int32, (CHUNK, N_KEYS), 1)
    dist = jnp.abs(WINDOW + qi - kj).astype(F32)
    half = lax.broadcasted_iota(jnp.int32, (2 * CHUNK, 1), 0) < CHUNK
    alibi, sink_col = [], []
    for kh in range(N_KV_HEADS):
        for pos in range(2):
            h_a = GQA_GROUP * kh + pos
            h_b = h_a + 2
            alibi.append(jnp.concatenate(
                [dist * (2.0 ** -(h_a + 1)), dist * (2.0 ** -(h_b + 1))], axis=0))
            sink_col.append(jnp.where(half, sink_ref[h_a], sink_ref[h_b]))
    key_idx = lax.broadcasted_iota(jnp.int32, (1, N_KEYS), 1)
    ti = lax.broadcasted_iota(jnp.int32, (CHUNK, CHUNK), 0)
    si = lax.broadcasted_iota(jnp.int32, (CHUNK, CHUNK), 1)
    causal = si <= ti

    assert n_chunks == 1, "stage B is written phase-major for one chunk per batch row"
    row0 = [b * ts for b in range(bb)]
    att_jobs = [(b, kh, pos) for b in range(bb) for kh in range(N_KV_HEADS) for pos in range(2)]
    jobs = [(b, h) for b in range(bb) for h in range(N_LSTM_HEADS)]
    hsl = lambda h: slice(h * LSTM_HEAD_DIM, (h + 1) * LSTM_HEAD_DIM)
    rsl = lambda b: slice(row0[b], row0[b] + CHUNK)

    for b in range(bb):
        kvar = _kv_variants(kn[row0[b]:row0[b] + ts], lo)
        vvar = _kv_variants(v[row0[b]:row0[b] + ts], lo)
        for i in range(4):
            kh_ref[i, b, WINDOW:WINDOW + ts, :] = kvar[i]
            vh_ref[i, b, WINDOW:WINDOW + ts, :] = vvar[i]
    scores = []
    for b, kh, pos in att_jobs:
        rs = row0[b]
        qs = jnp.concatenate(
            [qb[rs:rs + CHUNK, (2 * kh) * LANES:(2 * kh + 1) * LANES],
             qb[rs:rs + CHUNK, (2 * kh + 1) * LANES:(2 * kh + 2) * LANES]], axis=0)
        keys = kh_ref[2 * kh + pos, b, 0:N_KEYS, :]
        scores.append(lax.dot_general(qs, keys, _NT, preferred_element_type=F32))
    c_old = {j: c_ref[j[0], j[1]] for j in jobs}
    n_old = {j: n_ref[j[0], j[1]:j[1] + 1, :] for j in jobs}
    qk = {(b, h): lax.dot_general(mqb[rsl(b), hsl(h)], mkb[rsl(b), hsl(h)], _NT,
                                  preferred_element_type=F32) for b, h in jobs}
    q_c = {(b, h): _dot(mqb[rsl(b), hsl(h)], c_old[(b, h)].astype(BF16)) for b, h in jobs}
    emit_a(_A_BLOCKS_PER_PHASE[1])

    if not cfg.has_history:
        first_valid = (WINDOW // CHUNK - s_idx) * CHUNK
        alibi = [jnp.where(key_idx >= first_valid, al, -neg_inf) for al in alibi]
    probs, dens = [], []
    for (b, kh, pos), sc in zip(att_jobs, scores):
        i = 2 * kh + pos
        sc = sc - alibi[i]
        mx = jnp.maximum(jnp.max(sc, axis=-1, keepdims=True), sink_col[i])
        p = jnp.exp(sc - mx)
        dens.append(jnp.sum(p, axis=-1, keepdims=True) + jnp.exp(sink_col[i] - mx))
        probs.append(p.astype(BF16))
    m_rows = [m_ref[b] for b in range(bb)]
    bcs = [b_all[rsl(b)] for b in range(bb)]
    rcs = [r_all[rsl(b)] for b in range(bb)]
    inters, decays, wss, m_news = [], [], [], []
    for b in range(bb):
        tot = bcs[b][CHUNK - 1:CHUNK, :]
        g = tot + rcs[b]
        m_new = jnp.maximum(tot + m_rows[b], jnp.max(g, axis=0, keepdims=True))
        inters.append(bcs[b] + m_rows[b])
        decays.append(jnp.exp(tot + m_rows[b] - m_new))
        wss.append(jnp.exp(g - m_new))
        m_news.append(m_new)
    a_mat, mts, w_inters = {}, {}, {}
    for b, h in jobs:
        sel = lane == GATE_LANE0 + h
        b_col = _lane_column(bcs[b], sel)
        inter_col = _lane_column(inters[b], sel)
        r_row = r_rows[GATE_LANE0 + h:GATE_LANE0 + h + 1, rsl(b)]
        d = jnp.where(causal, b_col + r_row, neg_inf)
        mt = jnp.maximum(inter_col, jnp.max(d, axis=1, keepdims=True))
        a_mat[(b, h)] = jnp.exp(d - mt) * qk[(b, h)]
        mts[(b, h)] = mt
        w_inters[(b, h)] = jnp.exp(inter_col - mt)
    wks = {}
    for b, h in jobs:
        wks[(b, h)] = _lane_column(wss[b], lane == GATE_LANE0 + h) * mk[rsl(b), hsl(h)]

    att_o = [_dot(p, vh_ref[2 * kh + pos, b, 0:N_KEYS, :])
             for (b, kh, pos), p in zip(att_jobs, probs)]
    intra = {(b, h): _dot(a_mat[(b, h)].astype(BF16), mvb[rsl(b), hsl(h)]) for b, h in jobs}
    c_upd = {(b, h): lax.dot_general(wks[(b, h)].astype(BF16), mvb[rsl(b), hsl(h)], _TN,
                                     preferred_element_type=F32) for b, h in jobs}
    emit_a(_A_BLOCKS_PER_PHASE[2])

    pair_out = [[None] * 4 for _ in range(bb)]
    for (b, kh, pos), o, den in zip(att_jobs, att_o, dens):
        o = o / den
        for part in range(2):
            j = 2 * kh + part
            piece = o[part * CHUNK:(part + 1) * CHUNK]
            pair_out[b][j] = piece if pair_out[b][j] is None else pair_out[b][j] + piece
    att_rows = [jnp.concatenate(pair_out[b], axis=-1) for b in range(bb)]
    head_out = {}
    for b, h in jobs:
        j = (b, h)
        q_n = jnp.sum(mq[rsl(b), hsl(h)] * n_old[j], axis=1, keepdims=True)
        num = intra[j] + w_inters[j] * q_c[j]
        den = jnp.sum(a_mat[j], axis=1, keepdims=True) + w_inters[j] * q_n
        head_out[j] = num / jnp.maximum(jnp.abs(den), jnp.exp(-mts[j]))
    for b, h in jobs:
        j = (b, h)
        dec = _lane_column(decays[b], lane == GATE_LANE0 + h)
        c_ref[b, h] = dec * c_old[j] + c_upd[j]
        n_ref[b, h:h + 1, :] = dec * n_old[j] + jnp.sum(wks[j], axis=0, keepdims=True)
    hl_rows = [jnp.concatenate([head_out[(b, h)] for h in range(N_LSTM_HEADS)], axis=-1)
               for b in range(bb)]
    for b in range(bb):
        r0 = row0[b]
        m_ref[b] = m_news[b]
        if cfg.n_seq > 1:
            for i in range(4):
                kh_ref[i, b, 0:WINDOW, :] = kh_ref[i, b, ts:ts + WINDOW, :]
                vh_ref[i, b, 0:WINDOW, :] = vh_ref[i, b, ts:ts + WINDOW, :]
        ko_ref[b, 0:WINDOW - ts, :] = ko_ref[b, ts:WINDOW, :]
        vo_ref[b, 0:WINDOW - ts, :] = vo_ref[b, ts:WINDOW, :]
        ko_ref[b, WINDOW - ts:WINDOW, :] = kn[r0:r0 + ts]
        vo_ref[b, WINDOW - ts:WINDOW, :] = v[r0:r0 + ts]
    emit_a(_A_BLOCKS_PER_PHASE[3])

    att = jnp.concatenate(att_rows, axis=0).astype(BF16)
    hm = jnp.concatenate(hl_rows, axis=0)
    gated = jax.nn.sigmoid(mo) * hm
    glo = glo_ref[...]
    hl = jnp.concatenate(
        [_rms(gated[:, hsl(h)], glo[:, hsl(h)]) for h in range(N_LSTM_HEADS)],
        axis=-1).astype(BF16)
    mixed = jnp.concatenate([att, hl], axis=-1)
    x2_ref[...] = (x1 + _dot(mixed, wo_ref[...])).reshape(bb, ts, D_MODEL)

    stage_a_in_proj()
    assert next(a_blocks, None) is None


def _ffn_kernel(x_ref, g_ref, wg_ref, wu_ref, wd_ref, gfin_ref, y_ref):
    n_sub = _FFN_ROWS // _FFN_SUB_ROWS
    rows_of = lambda i: slice(i * _FFN_SUB_ROWS, (i + 1) * _FFN_SUB_ROWS)

    def gate_up(i):
        h = _rms(x_ref[rows_of(i), :], g_ref[...]).astype(BF16)
        gate = _dot(h, wg_ref[...])
        up = _dot(h, wu_ref[...])
        return gate, up

    def down(i, gate, up):
        act = (gate * jax.nn.sigmoid(gate) * up).astype(BF16)
        x3 = x_ref[rows_of(i), :] + 0.5 * _dot(act, wd_ref[...])
        y_ref[rows_of(i), :] = _rms(x3, gfin_ref[...])

    gu = gate_up(0)
    for i in range(n_sub):
        nxt = gate_up(i + 1) if i + 1 < n_sub else None
        down(i, *gu)
        gu = nxt


def _resident(shape):
    return pl.BlockSpec(shape, lambda *_: (0,) * len(shape), pipeline_mode=pl.Buffered(1))


def _run_mixer(cfg, x, history, sinks, weights):
    batch, seq, _ = x.shape
    bb, ts, n_seq, n_tiles = cfg.bb, cfg.ts, cfg.n_seq, cfg.n_tiles
    assert batch % bb == 0 and seq == ts * n_seq and ts % CHUNK == 0
    assert n_tiles == (batch // bb) * n_seq
    assert ts >= WINDOW or 2 * ts == WINDOW

    def tile_a(t):
        return jnp.minimum(t, n_tiles - 1)

    def tile_b(t):
        return jnp.maximum(t - 1, 0)

    def per_batch_b(*tail):
        mode = pl.Buffered(1) if n_seq > 1 else None
        return pl.BlockSpec((bb,) + tail, lambda t: (tile_b(t) // n_seq,) + (0,) * len(tail),
                            pipeline_mode=mode)

    state_specs = [
        per_batch_b(WINDOW, KV_WIDTH), per_batch_b(WINDOW, KV_WIDTH),
        per_batch_b(N_LSTM_HEADS, LSTM_HEAD_DIM, LSTM_HEAD_DIM),
        per_batch_b(N_LSTM_HEADS, LSTM_HEAD_DIM), per_batch_b(1, LANES),
    ]
    in_specs = [
        pl.BlockSpec(memory_space=pltpu.SMEM),
        pl.BlockSpec((bb, ts, D_MODEL), lambda t: (tile_a(t) // n_seq, tile_a(t) % n_seq, 0)),
    ] + (state_specs if cfg.has_history else []) + [_resident(w.shape) for w in weights]
    out_shape = (
        jax.ShapeDtypeStruct((batch, seq, D_MODEL), F32),
        jax.ShapeDtypeStruct((batch, WINDOW, KV_WIDTH), F32),
        jax.ShapeDtypeStruct((batch, WINDOW, KV_WIDTH), F32),
        jax.ShapeDtypeStruct((batch, N_LSTM_HEADS, LSTM_HEAD_DIM, LSTM_HEAD_DIM), F32),
        jax.ShapeDtypeStruct((batch, N_LSTM_HEADS, LSTM_HEAD_DIM), F32),
        jax.ShapeDtypeStruct((batch, 1, LANES), F32),
    )
    out_specs = [
        pl.BlockSpec((bb, ts, D_MODEL), lambda t: (tile_b(t) // n_seq, tile_b(t) % n_seq, 0)),
    ] + state_specs
    rows = bb * ts
    hist = pltpu.VMEM((4, bb, WINDOW + ts, LANES), BF16)
    return pl.pallas_call(
        functools.partial(_mixer_kernel, cfg),
        grid=(n_tiles + 1,),
        in_specs=in_specs, out_specs=out_specs, out_shape=out_shape,
        scratch_shapes=[hist, hist,
                        pltpu.VMEM((1, rows, D_MODEL), F32),
                        pltpu.VMEM((1, rows, _W_IN_PADDED), F32)],
        compiler_params=pltpu.CompilerParams(
            dimension_semantics=("arbitrary",), vmem_limit_bytes=_VMEM_LIMIT),
        name=f"mixer_{'sample' if cfg.has_history else 'prompt'}",
    )(sinks, x, *history, *weights)


def _run_ffn(x, weights, name):
    n_rows = x.shape[0]
    assert n_rows % _FFN_ROWS == 0
    row_spec = pl.BlockSpec((_FFN_ROWS, D_MODEL), lambda i: (i, 0))
    return pl.pallas_call(
        _ffn_kernel,
        grid=(n_rows // _FFN_ROWS,),
        in_specs=[row_spec] + [_resident(w.shape) for w in weights],
        out_specs=row_spec,
        out_shape=jax.ShapeDtypeStruct((n_rows, D_MODEL), F32),
        compiler_params=pltpu.CompilerParams(
            dimension_semantics=("arbitrary",), vmem_limit_bytes=_VMEM_LIMIT),
        name=name,
    )(x, *weights)


def _pad_gate_lanes(t):
    return jnp.pad(t, ((0, 0), (GATE_LANE0, LANES - GATE_LANE0 - N_LSTM_HEADS)))[:, None, :]


def kernel(x_prompt, x_sample, cache_k, cache_v, state_C, state_n, state_m, g_ffn1, w1_gate, w1_up, w1_down, g_mix, w_in, b_igate, b_fgate, g_q, g_k, attn_sinks, g_lstm_out, w_out, g_ffn2, w2_gate, w2_up, w2_down, g_final):
    assert w_in.shape[0] == 1, "one layer"
    batch, seq, _ = x_prompt.shape
    dec_batch, dec_seq, _ = x_sample.shape
    row = lambda t: t.reshape(1, -1)
    mat = lambda t: t.reshape(t.shape[1:]).astype(BF16)
    n_gate = 2 * N_LSTM_HEADS
    win_all = w_in.reshape(D_MODEL, -1)
    win = win_all[:, :_OFF_GATE].astype(BF16)
    wgate = jnp.pad(win_all[:, _OFF_GATE:], ((0, 0), (0, LANES - n_gate))).astype(BF16)
    gbias = jnp.pad(jnp.concatenate([b_igate[0], b_fgate[0]]), (0, LANES - n_gate)).reshape(1, LANES)
    mixer_weights = (
        row(g_ffn1), mat(w1_gate), mat(w1_up), mat(w1_down),
        row(g_mix), win, wgate, gbias,
        row(jnp.tile(g_q[0], N_ATT_HEADS)), row(jnp.tile(g_k[0], N_KV_HEADS)),
        row(g_lstm_out), mat(w_out),
    )
    ffn_weights = (row(g_ffn2), mat(w2_gate), mat(w2_up), mat(w2_down), row(g_final))
    sinks = attn_sinks.reshape(N_ATT_HEADS)

    bb = _PROMPT_BATCH_ROWS
    prompt_cfg = _Tile(bb=bb, ts=CHUNK, n_seq=seq // CHUNK, n_tiles=(batch // bb) * (seq // CHUNK),
                       has_history=False)
    x2p, kp, vp, cp, n_p, mp = _run_mixer(prompt_cfg, x_prompt, (), sinks, mixer_weights)
    yp = _run_ffn(x2p.reshape(batch * seq, D_MODEL), ffn_weights, "ffn_prompt")

    rows = cache_k.shape[2]
    assert rows == WINDOW
    bb = _SAMPLE_BATCH_ROWS
    sample_cfg = _Tile(bb=bb, ts=dec_seq, n_seq=1, n_tiles=dec_batch // bb, has_history=True)
    history = (cache_k.reshape(dec_batch, rows, KV_WIDTH),
               cache_v.reshape(dec_batch, rows, KV_WIDTH),
               state_C.reshape(state_C.shape[1:]), state_n.reshape(state_n.shape[1:]),
               _pad_gate_lanes(state_m.reshape(dec_batch, N_LSTM_HEADS)))
    x2s, ks, vs, cs, n_s, ms = _run_mixer(sample_cfg, x_sample, history, sinks, mixer_weights)
    ys = _run_ffn(x2s.reshape(dec_batch * dec_seq, D_MODEL), ffn_weights, "ffn_sample")

    kv_shape = lambda b: (1, b, rows, N_KV_HEADS, HEAD_DIM)
    m_out = lambda t: t[:, 0, GATE_LANE0:GATE_LANE0 + N_LSTM_HEADS][None]
    return (yp.reshape(batch, seq, D_MODEL), ys.reshape(dec_batch, dec_seq, D_MODEL),
            kp.reshape(kv_shape(batch)), vp.reshape(kv_shape(batch)),
            cp[None], n_p[None], m_out(mp),
            ks.reshape(kv_shape(dec_batch)), vs.reshape(kv_shape(dec_batch)),
            cs[None], n_s[None], m_out(ms))
```
